```python
import math, functools
import jax, jax.numpy as jnp
from jax import lax
import numpy as np

D_MODEL = 1024
BATCH = 8
SEQ = 2048
DEPTH = 2
DEC_BATCH = 32
DEC_SEQ = 4
PAST_LEN = 16384
PAGE_SIZE = 128

R_HEADS = 4
R_DK = 64
R_DV = 128
R_QK = R_HEADS * R_DK
R_WIDTH = R_HEADS * R_DV
F_HEADS = 8
F_DH = 64
F_WIDTH = F_HEADS * F_DH
M_HEADS = 8
M_HD = 64
M_WIDTH = M_HEADS * M_HD
M_GROUPS = 2
M_STATE = 128
M_CONV = 4
M_CONV_DIM = M_WIDTH + 2 * M_GROUPS * M_STATE

CHUNK = 128
Q_BLOCK = 128
ROPE_BASE = 10000.0
EPS = 1e-6
F_BIAS_INIT = 3.0

IN_SIZES = (R_QK, R_QK, R_WIDTH, R_WIDTH,
            F_WIDTH, F_WIDTH, F_WIDTH, F_HEADS, F_WIDTH,
            M_CONV_DIM, M_HEADS, M_WIDTH,
            3 * D_MODEL)
IN_WIDTH = sum(IN_SIZES)

kernel_name = 'hybrid_retention_fox_ssd_step'

F32 = jnp.float32


def rmsnorm(x, g):
    xf = x.astype(F32)
    r = xf * lax.rsqrt(jnp.mean(xf * xf, axis=-1, keepdims=True) + EPS)
    return (r * g).astype(x.dtype)


def group_norm(o, g):
    of = o.astype(F32)
    mu = jnp.mean(of, axis=-1, keepdims=True)
    var = jnp.mean((of - mu) ** 2, axis=-1, keepdims=True)
    out = ((of - mu) * lax.rsqrt(var + EPS)).reshape(o.shape[0], o.shape[1], -1) * g
    return out.astype(o.dtype)


def rope(x, pos):
    d = x.shape[-1]
    inv = ROPE_BASE ** (-jnp.arange(0, d, 2, dtype=F32) / d)
    ang = pos.astype(F32)[:, None] * inv[None, :]
    cos = jnp.cos(ang)[None, :, None, :]
    sin = jnp.sin(ang)[None, :, None, :]
    x1, x2 = x[..., : d // 2].astype(F32), x[..., d // 2:].astype(F32)
    return jnp.concatenate([x1 * cos - x2 * sin, x1 * sin + x2 * cos], axis=-1).astype(x.dtype)


def retention(q, k, v, s0):
    B, L, H, DK = q.shape
    C = math.gcd(L, CHUNK)
    n = L // C
    log_g = jnp.log1p(-jnp.exp2(-5.0 - jnp.arange(H, dtype=F32)))
    qc = q.reshape(B, n, C, H, DK) * (DK ** -0.5)
    kc = k.reshape(B, n, C, H, DK)
    vc = v.reshape(B, n, C, H, -1)
    i = jnp.arange(C, dtype=F32)
    diff = i[:, None] - i[None, :]
    decay = jnp.where(diff >= 0, jnp.exp(log_g[:, None, None] * jnp.maximum(diff, 0.0)), 0.0)
    scores = jnp.einsum('bnihd,bnjhd->bnhij', qc, kc) * decay
    o_intra = jnp.einsum('bnhij,bnjhe->bnihe', scores, vc)
    w_k = jnp.exp(log_g[None, :] * (C - 1 - i)[:, None])
    d_state = jnp.einsum('bnjhd,jh,bnjhe->bnhde', kc, w_k, vc).astype(F32)
    g_chunk = jnp.exp(log_g * C)[None, :, None, None]

    def step(S, dS):
        return g_chunk * S + dS, S

    s_last, s_prev = lax.scan(step, s0.astype(F32), jnp.moveaxis(d_state, 1, 0))
    s_prev = jnp.moveaxis(s_prev, 0, 1)
    w_q = jnp.exp(log_g[None, :] * (i + 1.0)[:, None])
    o_cross = jnp.einsum('bnihd,ih,bnhde->bnihe', qc, w_q, s_prev)
    o = (o_intra + o_cross).reshape(B, L, H, -1).astype(q.dtype)
    return o, s_last.astype(s0.dtype)


def fox_attend(q, k, v, cq, ck, qpos, kpos):
    s = jnp.einsum('bqhd,bkhd->bhqk', q, k).astype(F32) * (q.shape[-1] ** -0.5)
    s = s + jnp.transpose(cq, (0, 2, 1))[..., None] - jnp.transpose(ck, (0, 2, 1))[:, :, None, :]
    s = jnp.where(kpos[None, :] <= qpos[:, None], s, -1e30)
    p = jax.nn.softmax(s, axis=-1)
    return jnp.einsum('bhqk,bkhd->bqhd', p.astype(v.dtype), v)


def fox_prompt(q, k, v, logf):
    B, L, H, D = q.shape
    blk = math.gcd(L, Q_BLOCK)
    nb = L // blk
    c = jnp.cumsum(logf.astype(F32), axis=1)
    kpos = jnp.arange(L)
    qb = jnp.moveaxis(q.reshape(B, nb, blk, H, D), 1, 0)
    cb = jnp.moveaxis(c.reshape(B, nb, blk, H), 1, 0)

    def one(args):
        qi, ci, bi = args
        qpos = bi * blk + jnp.arange(blk)
        return fox_attend(qi, k, v, ci, c, qpos, kpos)

    out = lax.map(one, (qb, cb, jnp.arange(nb)))
    return jnp.moveaxis(out, 0, 1).reshape(B, L, H, D)


def fox_sample(q, k, v, logf, k_cache, v_cache, logf_cache, page_table, layer):
    Bd, T, H, D = q.shape
    past = page_table.shape[1] * PAGE_SIZE
    kp = k_cache[layer, page_table].reshape(Bd, past, H, D)
    vp = v_cache[layer, page_table].reshape(Bd, past, H, D)
    lp = logf_cache[layer, page_table].reshape(Bd, past, H)
    k_all = jnp.concatenate([kp, k.astype(kp.dtype)], axis=1)
    v_all = jnp.concatenate([vp, v.astype(vp.dtype)], axis=1)
    c_all = jnp.cumsum(jnp.concatenate([lp.astype(F32), logf.astype(F32)], axis=1), axis=1)
    qpos = past + jnp.arange(T)
    kpos = jnp.arange(past + T)
    return fox_attend(q, k_all, v_all, c_all[:, past:], c_all, qpos, kpos)


def causal_conv(xbc, buf, w, b):
    L = xbc.shape[1]
    xp = jnp.concatenate([buf.astype(xbc.dtype), xbc], axis=1)
    out = b + sum(xp[:, j:j + L] * w[j] for j in range(M_CONV))
    return jax.nn.silu(out), xp[:, -(M_CONV - 1):]


def ssd(x, dt, A, Bm, Cm, h0):
    Bsz, L, H, P = x.shape
    G, N = Bm.shape[2], Bm.shape[3]
    R = H // G
    C = math.gcd(L, CHUNK)
    n = L // C
    xs = x.reshape(Bsz, n, C, G, R, P)
    dts = dt.reshape(Bsz, n, C, G, R)
    a_cs = jnp.cumsum(dts * A.reshape(G, R), axis=2)
    Bc = Bm.reshape(Bsz, n, C, G, N)
    Cc = Cm.reshape(Bsz, n, C, G, N)
    seg = a_cs[:, :, :, None] - a_cs[:, :, None, :]
    causal = jnp.tril(jnp.ones((C, C), dtype=bool))[:, :, None, None]
    lmat = jnp.exp(jnp.where(causal, seg, -jnp.inf))
    cb = jnp.einsum('bcigs,bcjgs->bcijg', Cc, Bc)
    y_intra = jnp.einsum('bcijg,bcijgr,bcjgr,bcjgrp->bcigrp', cb, lmat, dts, xs)
    w_end = jnp.exp(a_cs[:, :, -1:] - a_cs) * dts
    d_state = jnp.einsum('bcjgr,bcjgrp,bcjgs->bcgrps', w_end, xs, Bc).astype(F32)
    chunk_decay = jnp.exp(a_cs[:, :, -1])[..., None, None]

    def step(h, inp):
        dec, ds = inp
        return dec * h + ds, h

    h_last, h_prev = lax.scan(step, h0.reshape(Bsz, G, R, P, N).astype(F32),
                              (jnp.moveaxis(chunk_decay, 1, 0), jnp.moveaxis(d_state, 1, 0)))
    h_prev = jnp.moveaxis(h_prev, 0, 1)
    y_cross = jnp.einsum('bcigs,bcigr,bcgrps->bcigrp', Cc, jnp.exp(a_cs), h_prev)
    y = (y_intra + y_cross).reshape(Bsz, L, H, P)
    return y, h_last.reshape(Bsz, H, P, N).astype(h0.dtype)


def mixer_layer(x, pos, ret_s0, conv_buf, ssm_h0, attend,
                norm_g, w_in, r_norm_g, f_qnorm_g, f_knorm_g, f_bias,
                m_conv_w, m_conv_b, m_dt_bias, m_A_log, m_D, m_norm_g,
                w_br_r, w_br_f, w_br_m, w_out):
    B, L, _ = x.shape
    h = rmsnorm(x, norm_g)
    z = h @ w_in
    split_points = np.cumsum(IN_SIZES)[:-1].tolist()
    (rq, rk, rv, rg, fq, fk, fv, ff, fg, mxbc, mdt, mz, gates) = jnp.split(z, split_points, axis=-1)

    rq = rope(rq.reshape(B, L, R_HEADS, R_DK), pos)
    rk = rope(rk.reshape(B, L, R_HEADS, R_DK), pos)
    ro, ret_s = retention(rq, rk, rv.reshape(B, L, R_HEADS, R_DV), ret_s0)
    r_out = group_norm(ro, r_norm_g) * jax.nn.silu(rg)

    fq = rmsnorm(fq.reshape(B, L, F_HEADS, F_DH), f_qnorm_g)
    fk = rmsnorm(fk.reshape(B, L, F_HEADS, F_DH), f_knorm_g)
    fv = fv.reshape(B, L, F_HEADS, F_DH)
    logf = jax.nn.log_sigmoid((ff + f_bias).astype(F32))
    fo = attend(fq, fk, fv, logf)
    f_out = fo.reshape(B, L, F_WIDTH) * jax.nn.silu(fg)

    xbc, conv_new = causal_conv(mxbc, conv_buf, m_conv_w, m_conv_b)
    mx, mB, mC = jnp.split(xbc, [M_WIDTH, M_WIDTH + M_GROUPS * M_STATE], axis=-1)
    dt = jax.nn.softplus((mdt + m_dt_bias).astype(F32))
    A = -jnp.exp(m_A_log.astype(F32))
    mx = mx.reshape(B, L, M_HEADS, M_HD)
    my, ssm_h = ssd(mx, dt, A, mB.reshape(B, L, M_GROUPS, M_STATE),
                    mC.reshape(B, L, M_GROUPS, M_STATE), ssm_h0)
    my = (my + m_D[:, None] * mx).astype(x.dtype).reshape(B, L, M_WIDTH)
    m_out = rmsnorm(my * jax.nn.silu(mz), m_norm_g)

    g_r, g_f, g_m = jnp.split(jax.nn.sigmoid(gates), 3, axis=-1)
    merged = g_r * (r_out @ w_br_r) + g_f * (f_out @ w_br_f) + g_m * (m_out @ w_br_m)
    y = x + (merged @ w_out).astype(x.dtype)
    return y, (ret_s, fk, fv, logf.astype(x.dtype), conv_new, ssm_h)


def setup_inputs(seed: int = 0) -> dict:
    key = jax.random.key(seed)
    ks = jax.random.split(key, 32)
    n_pages = PAST_LEN // PAGE_SIZE
    n_used = DEC_BATCH * n_pages
    n_pool = n_used + max(1, n_used // 4)
    nrm = lambda k, shape, s=1.0: jax.random.normal(k, shape, F32) * s
    page_table = jax.random.permutation(ks[0], n_pool)[:n_used].reshape(DEC_BATCH, n_pages).astype(jnp.int32)
    dt0 = jax.random.uniform(ks[1], (DEPTH, M_HEADS), F32, 0.001, 0.1)
    return {
        'x_prompt': nrm(ks[2], (BATCH, SEQ, D_MODEL)),
        'x_sample': nrm(ks[3], (DEC_BATCH, DEC_SEQ, D_MODEL)),
        'cache_fox_k': nrm(ks[4], (DEPTH, n_pool, PAGE_SIZE, F_HEADS, F_DH)),
        'cache_fox_v': nrm(ks[5], (DEPTH, n_pool, PAGE_SIZE, F_HEADS, F_DH)),
        'cache_fox_logf': jax.nn.log_sigmoid(F_BIAS_INIT + nrm(ks[6], (DEPTH, n_pool, PAGE_SIZE, F_HEADS))),
        'page_table': page_table,
        'state_ret': nrm(ks[7], (DEPTH, DEC_BATCH, R_HEADS, R_DK, R_DV)),
        'state_conv': nrm(ks[8], (DEPTH, DEC_BATCH, M_CONV - 1, M_CONV_DIM)),
        'state_ssm': nrm(ks[9], (DEPTH, DEC_BATCH, M_HEADS, M_HD, M_STATE), 0.1),
        'norm_g': 1.0 + nrm(ks[10], (DEPTH, D_MODEL), 0.02),
        'w_in': nrm(ks[11], (DEPTH, D_MODEL, IN_WIDTH), D_MODEL ** -0.5),
        'r_norm_g': 1.0 + nrm(ks[12], (DEPTH, R_WIDTH), 0.02),
        'f_qnorm_g': 1.0 + nrm(ks[13], (DEPTH, F_DH), 0.02),
        'f_knorm_g': 1.0 + nrm(ks[14], (DEPTH, F_DH), 0.02),
        'f_bias': F_BIAS_INIT + nrm(ks[15], (DEPTH, F_HEADS), 0.1),
        'm_conv_w': nrm(ks[16], (DEPTH, M_CONV, M_CONV_DIM), M_CONV ** -0.5),
        'm_conv_b': nrm(ks[17], (DEPTH, M_CONV_DIM), 0.01),
        'm_dt_bias': jnp.log(jnp.expm1(dt0)),
        'm_A_log': jnp.log(jax.random.uniform(ks[18], (DEPTH, M_HEADS), F32, 1.0, 16.0)),
        'm_D': 1.0 + nrm(ks[19], (DEPTH, M_HEADS), 0.02),
        'm_norm_g': 1.0 + nrm(ks[20], (DEPTH, M_WIDTH), 0.02),
        'w_br_r': nrm(ks[21], (DEPTH, R_WIDTH, D_MODEL), R_WIDTH ** -0.5),
        'w_br_f': nrm(ks[22], (DEPTH, F_WIDTH, D_MODEL), F_WIDTH ** -0.5),
        'w_br_m': nrm(ks[23], (DEPTH, M_WIDTH, D_MODEL), M_WIDTH ** -0.5),
        'w_out': nrm(ks[24], (DEPTH, D_MODEL, D_MODEL), D_MODEL ** -0.5),
    }


def reference(x_prompt, x_sample, cache_fox_k, cache_fox_v, cache_fox_logf, page_table,
              state_ret, state_conv, state_ssm,
              norm_g, w_in, r_norm_g, f_qnorm_g, f_knorm_g, f_bias,
              m_conv_w, m_conv_b, m_dt_bias, m_A_log, m_D, m_norm_g,
              w_br_r, w_br_f, w_br_m, w_out):
    B, L, _ = x_prompt.shape
    Bd, T, _ = x_sample.shape
    past = page_table.shape[1] * PAGE_SIZE
    pos_p = jnp.arange(L)
    pos_s = past + jnp.arange(T)
    dt_ = x_prompt.dtype
    hp, hs = x_prompt, x_sample
    new_p, new_s = [], []
    for l in range(DEPTH):
        weights = (norm_g[l], w_in[l], r_norm_g[l], f_qnorm_g[l], f_knorm_g[l], f_bias[l],
                   m_conv_w[l], m_conv_b[l], m_dt_bias[l], m_A_log[l], m_D[l], m_norm_g[l],
                   w_br_r[l], w_br_f[l], w_br_m[l], w_out[l])
        hp, st_p = mixer_layer(hp, pos_p,
                               jnp.zeros((B, R_HEADS, R_DK, R_DV), dt_),
                               jnp.zeros((B, M_CONV - 1, M_CONV_DIM), dt_),
                               jnp.zeros((B, M_HEADS, M_HD, M_STATE), dt_),
                               fox_prompt, *weights)
        attend_s = functools.partial(fox_sample, k_cache=cache_fox_k, v_cache=cache_fox_v,
                                     logf_cache=cache_fox_logf, page_table=page_table, layer=l)
        hs, st_s = mixer_layer(hs, pos_s, state_ret[l], state_conv[l], state_ssm[l],
                               attend_s, *weights)
        new_p.append(st_p)
        new_s.append(st_s)
    stk = lambda sts, i: jnp.stack([s[i] for s in sts])
    return (hp, hs,
            stk(new_p, 1), stk(new_p, 2), stk(new_p, 3), stk(new_p, 0), stk(new_p, 4), stk(new_p, 5),
            stk(new_s, 1), stk(new_s, 2), stk(new_s, 3), stk(new_s, 0), stk(new_s, 4), stk(new_s, 5))
```

```python
import functools
import math

import numpy as np
import jax
import jax.numpy as jnp
from jax import lax
from jax.experimental import pallas as pl
from jax.experimental.pallas import tpu as pltpu

F32 = jnp.float32
BF16 = jnp.bfloat16

D_MODEL = 1024
PAGE = 128
R_HEADS, R_DK, R_DV = 4, 64, 128
F_HEADS, F_DH = 8, 64
F_WIDTH = F_HEADS * F_DH
M_HEADS, M_HD, M_GROUPS, M_STATE, M_CONV = 8, 64, 2, 128, 4
M_WIDTH = M_HEADS * M_HD
M_CONV_DIM = M_WIDTH + 2 * M_GROUPS * M_STATE
CHUNK = 128
ROPE_BASE = 10000.0
EPS = 1e-6
NEG = -1e30

Z_RET, Z_MAM, Z_GATE, Z_FOX = 0, 1536, 3072, 6144
Z_WIDTH = 8192
SMALL_W = 256
LANES = 128
SUB = 8
TPAD = 8

VMEM_LIMIT = 56 * 1024 * 1024

NT_DIMS = (((1,), (1,)), ((), ()))
TN_DIMS = (((0,), (0,)), ((), ()))

LOG_G = [float(np.log1p(-np.exp2(-5.0 - h))) for h in range(R_HEADS)]


def _cparams(sem):
    return pltpu.CompilerParams(dimension_semantics=sem, vmem_limit_bytes=VMEM_LIMIT)


def _silu(x):
    return x * jax.nn.sigmoid(x)


def _softplus(x):
    return jnp.maximum(x, 0.0) + jnp.log1p(jnp.exp(-jnp.abs(x)))


def _log_sigmoid(x):
    return jnp.minimum(x, 0.0) - jnp.log1p(jnp.exp(-jnp.abs(x)))


def _bdot(a, b):
    return jnp.dot(a.astype(BF16), b.astype(BF16), preferred_element_type=F32)


def _bdot_nt(a, b):
    return lax.dot_general(a.astype(BF16), b.astype(BF16), NT_DIMS, preferred_element_type=F32)


def _bdot_tn(a, b):
    return lax.dot_general(a.astype(BF16), b.astype(BF16), TN_DIMS, preferred_element_type=F32)


def _split3(x):
    h1 = x.astype(BF16)
    r1 = x - h1.astype(F32)
    h2 = r1.astype(BF16)
    h3 = (r1 - h2.astype(F32)).astype(BF16)
    return h1, h2, h3


def _exact_dot(x, sel):
    h1, h2, h3 = _split3(x)
    d = lambda a: jnp.dot(a, sel, preferred_element_type=F32)
    return d(h1) + d(h2) + d(h3)


def _exact_rows(sel, x):
    h1, h2, h3 = _split3(x)
    d = lambda a: lax.dot_general(sel, a, NT_DIMS, preferred_element_type=F32)
    return d(h1) + d(h2) + d(h3)


def _cumsum_rows(x):
    n = x.shape[0]
    i = lax.broadcasted_iota(jnp.int32, x.shape, 0)
    s = 1
    while s < n:
        x = x + jnp.where(i >= s, pltpu.roll(x, s, 0), 0.0)
        s *= 2
    return x


def _eye_rows(r, k):
    return (lax.broadcasted_iota(jnp.int32, (r, k), 0) == lax.broadcasted_iota(jnp.int32, (r, k), 1)).astype(BF16)


def _inproj_kernel(x_ref, g_ref, w_ref, ws_ref, z_ref, s_ref, h_ref):
    @pl.when(pl.program_id(1) == 0)
    def _():
        x = x_ref[...]
        r = x * lax.rsqrt(jnp.mean(x * x, axis=-1, keepdims=True) + EPS)
        h = (r * g_ref[...]).astype(BF16)
        h_ref[...] = h
        s_ref[...] = jnp.dot(h, ws_ref[...], preferred_element_type=F32)

    z_ref[...] = jnp.dot(h_ref[...], w_ref[...], preferred_element_type=F32)


def _inproj(x, g, w_main, w_small):
    nt = x.shape[0]
    tm = min(nt, 1024)
    tn = 1024
    return pl.pallas_call(
        _inproj_kernel,
        grid=(nt // tm, Z_WIDTH // tn),
        in_specs=[
            pl.BlockSpec((tm, D_MODEL), lambda i, j: (i, 0)),
            pl.BlockSpec((1, D_MODEL), lambda i, j: (0, 0)),
            pl.BlockSpec((D_MODEL, tn), lambda i, j: (0, j)),
            pl.BlockSpec((D_MODEL, SMALL_W), lambda i, j: (0, 0)),
        ],
        out_specs=[
            pl.BlockSpec((tm, tn), lambda i, j: (i, j)),
            pl.BlockSpec((tm, SMALL_W), lambda i, j: (i, 0)),
        ],
        out_shape=[jax.ShapeDtypeStruct((nt, Z_WIDTH), F32), jax.ShapeDtypeStruct((nt, SMALL_W), F32)],
        scratch_shapes=[pltpu.VMEM((tm, D_MODEL), BF16)],
        compiler_params=_cparams(("arbitrary", "arbitrary")),
        name="inproj",
    )(x, g, w_main, w_small)


def _ret_kernel(z_ref, cos_ref, sin_ref, gn_ref, s0_ref, o_ref, sl_ref, st_ref, *, C, T):
    c = pl.program_id(1)

    @pl.when(c == 0)
    def _():
        st_ref[...] = s0_ref[...]

    lane2 = lax.broadcasted_iota(jnp.int32, (C, 2 * LANES), 1)
    first = (lane2 % R_DK) < (R_DK // 2)
    cos = cos_ref[...]
    sin = sin_ref[...]

    def rope(x):
        xr = jnp.where(first, pltpu.roll(x, 2 * LANES - R_DK // 2, 1), pltpu.roll(x, R_DK // 2, 1))
        return x * cos + xr * sin

    q = rope(z_ref[:, 0:256]) * (R_DK ** -0.5)
    k = rope(z_ref[:, 256:512])

    ri = lax.broadcasted_iota(jnp.int32, (C, 1), 0)
    rif = ri.astype(F32)
    valid = ri < T
    di = lax.broadcasted_iota(jnp.int32, (C, C), 0)
    dj = lax.broadcasted_iota(jnp.int32, (C, C), 1)
    dd = jnp.maximum(di - dj, 0).astype(F32)
    causal = di >= dj
    lane = lax.broadcasted_iota(jnp.int32, (C, LANES), 1)
    lo = lane < R_DK
    row = lax.broadcasted_iota(jnp.int32, (LANES, LANES), 0)
    rlo = row < R_DK

    for p in range(R_HEADS // 2):
        qp = q[:, p * LANES:(p + 1) * LANES]
        kp = jnp.where(valid, k[:, p * LANES:(p + 1) * LANES], 0.0)
        sp = st_ref[p]
        ds = []
        for e in range(2):
            h = 2 * p + e
            lg = LOG_G[h]
            qm = jnp.where(lo if e == 0 else jnp.logical_not(lo), qp, 0.0)
            decay = jnp.where(causal, jnp.exp(lg * dd), 0.0)
            wq = jnp.exp(lg * (rif + 1.0))
            wk = jnp.exp(lg * (float(T - 1) - rif))
            vh = z_ref[:, 512 + h * LANES:512 + (h + 1) * LANES]
            s = _bdot_nt(qm, kp) * decay
            o = _bdot(s, vh) + _bdot(qm * wq, sp)
            mu = jnp.mean(o, axis=-1, keepdims=True)
            oc = o - mu
            var = jnp.mean(oc * oc, axis=-1, keepdims=True)
            gate = z_ref[:, 1024 + h * LANES:1024 + (h + 1) * LANES]
            o_ref[:, h * LANES:(h + 1) * LANES] = (
                oc * lax.rsqrt(var + EPS) * gn_ref[:, h * LANES:(h + 1) * LANES] * _silu(gate))
            ds.append(_bdot_tn(kp * wk, vh))
        gdec = jnp.where(rlo, math.exp(LOG_G[2 * p] * T), math.exp(LOG_G[2 * p + 1] * T))
        st_ref[p] = gdec * sp + jnp.where(rlo, ds[0], ds[1])

    @pl.when(c == pl.num_programs(1) - 1)
    def _():
        sl_ref[...] = st_ref[...]


def _retention(z, zcol, cos, sin, gn, s0, *, B, n, C, T):
    nt = B * n * C
    kern = functools.partial(_ret_kernel, C=C, T=T)
    return pl.pallas_call(
        kern,
        grid=(B, n),
        in_specs=[
            pl.BlockSpec((C, 1536), lambda b, c: (b * n + c, zcol)),
            pl.BlockSpec((C, 256), lambda b, c: (c, 0)),
            pl.BlockSpec((C, 256), lambda b, c: (c, 0)),
            pl.BlockSpec((1, 512), lambda b, c: (0, 0)),
            pl.BlockSpec((None, 2, LANES, LANES), lambda b, c: (b, 0, 0, 0)),
        ],
        out_specs=[
            pl.BlockSpec((C, 512), lambda b, c: (b * n + c, 0)),
            pl.BlockSpec((None, 2, LANES, LANES), lambda b, c: (b, 0, 0, 0)),
        ],
        out_shape=[jax.ShapeDtypeStruct((nt, 512), F32), jax.ShapeDtypeStruct((B, 2, LANES, LANES), F32)],
        scratch_shapes=[pltpu.VMEM((2, LANES, LANES), F32)],
        compiler_params=_cparams(("arbitrary", "arbitrary")),
        name="retention",
    )(z, cos, sin, gn, s0)


def _mamba_kernel(z_ref, dt_ref, cw_ref, cb_ref, dtb_ref, alog_ref, dvec_ref, ng_ref, cbuf_ref, h0_ref,
                  o_ref, cnew_ref, hl_ref, xp_ref, st_ref, *, C, T):
    c = pl.program_id(1)
    K = M_CONV

    @pl.when(c == 0)
    def _():
        xp_ref[SUB - (K - 1):SUB, :] = cbuf_ref[...]
        st_ref[...] = h0_ref[...]

    xp_ref[SUB:SUB + C, :] = z_ref[:, 0:M_CONV_DIM]
    conv = cb_ref[...]
    for j in range(K):
        conv = conv + xp_ref[SUB - (K - 1) + j:SUB - (K - 1) + j + C, :] * cw_ref[j:j + 1, :]
    tail = xp_ref[SUB + T - (K - 1):SUB + T, :]
    xp_ref[SUB - (K - 1):SUB, :] = tail
    xbc = _silu(conv)

    lane = lax.broadcasted_iota(jnp.int32, (C, LANES), 1)
    ri = lax.broadcasted_iota(jnp.int32, (C, LANES), 0)
    dt = _softplus(dt_ref[...] + dtb_ref[...])
    dt = jnp.where((lane < M_HEADS) & (ri < T), dt, 0.0)
    a = dt * (-jnp.exp(alog_ref[...]))
    a_cs = _cumsum_rows(a)
    eye8 = _eye_rows(SUB, LANES)
    a_row = _exact_rows(eye8, a_cs)
    dt_row = _exact_rows(eye8, dt)
    a_last = a_cs[C - 1:C, :]
    w_end = jnp.exp(a_last - a_cs) * dt
    ea = jnp.exp(a_cs)
    e_last = jnp.exp(a_last)

    di = lax.broadcasted_iota(jnp.int32, (C, C), 0)
    dj = lax.broadcasted_iota(jnp.int32, (C, C), 1)
    causal = di >= dj
    lo = lane < M_HD
    rlo = lax.broadcasted_iota(jnp.int32, (LANES, LANES), 0) < M_HD

    ys = []
    heads_per_group = M_HEADS // M_GROUPS
    for g in range(M_GROUPS):
        bg = xbc[:, M_WIDTH + g * M_STATE:M_WIDTH + (g + 1) * M_STATE]
        cg = xbc[:, M_WIDTH + (M_GROUPS + g) * M_STATE:M_WIDTH + (M_GROUPS + g + 1) * M_STATE]
        cbm = _bdot_nt(cg, bg)
        for pp in range(heads_per_group // 2):
            p = g * (heads_per_group // 2) + pp
            xpair = xbc[:, p * LANES:(p + 1) * LANES]
            yh = []
            for e in range(2):
                h = 2 * p + e
                seg = a_cs[:, h:h + 1] - a_row[h:h + 1, :]
                lm = jnp.exp(jnp.where(causal, seg, NEG))
                m = cbm * lm * dt_row[h:h + 1, :]
                yh.append(_bdot(m, xpair))
            sp = st_ref[p]
            ycross = _bdot_nt(cg, sp) * jnp.where(lo, ea[:, 2 * p:2 * p + 1], ea[:, 2 * p + 1:2 * p + 2])
            y = jnp.where(lo, yh[0], yh[1]) + ycross + dvec_ref[:, p * LANES:(p + 1) * LANES] * xpair
            ys.append(y)
            xs = xpair * jnp.where(lo, w_end[:, 2 * p:2 * p + 1], w_end[:, 2 * p + 1:2 * p + 2])
            dstate = _bdot_tn(xs, bg)
            dec = jnp.where(rlo, e_last[:, 2 * p:2 * p + 1], e_last[:, 2 * p + 1:2 * p + 2])
            st_ref[p] = dec * sp + dstate

    my = jnp.concatenate(ys, axis=1)
    gated = my * _silu(z_ref[:, M_CONV_DIM:M_CONV_DIM + M_WIDTH])
    o_ref[...] = gated * lax.rsqrt(jnp.mean(gated * gated, axis=-1, keepdims=True) + EPS) * ng_ref[...]

    @pl.when(c == pl.num_programs(1) - 1)
    def _():
        cnew_ref[...] = xp_ref[SUB - (K - 1):SUB, :]
        hl_ref[...] = st_ref[...]


def _mamba(z, zcol, small, scol, cw, cb, dtb, alog, dvec, ng, cbuf, h0, *, B, n, C, T):
    nt = B * n * C
    kern = functools.partial(_mamba_kernel, C=C, T=T)
    np_ = M_HEADS // 2
    return pl.pallas_call(
        kern,
        grid=(B, n),
        in_specs=[
            pl.BlockSpec((C, 1536), lambda b, c: (b * n + c, zcol)),
            pl.BlockSpec((C, LANES), lambda b, c: (b * n + c, scol)),
            pl.BlockSpec((M_CONV, M_CONV_DIM), lambda b, c: (0, 0)),
            pl.BlockSpec((1, M_CONV_DIM), lambda b, c: (0, 0)),
            pl.BlockSpec((1, LANES), lambda b, c: (0, 0)),
            pl.BlockSpec((1, LANES), lambda b, c: (0, 0)),
            pl.BlockSpec((1, M_WIDTH), lambda b, c: (0, 0)),
            pl.BlockSpec((1, M_WIDTH), lambda b, c: (0, 0)),
            pl.BlockSpec((None, M_CONV - 1, M_CONV_DIM), lambda b, c: (b, 0, 0)),
            pl.BlockSpec((None, np_, LANES, LANES), lambda b, c: (b, 0, 0, 0)),
        ],
        out_specs=[
            pl.BlockSpec((C, M_WIDTH), lambda b, c: (b * n + c, 0)),
            pl.BlockSpec((None, M_CONV - 1, M_CONV_DIM), lambda b, c: (b, 0, 0)),
            pl.BlockSpec((None, np_, LANES, LANES), lambda b, c: (b, 0, 0, 0)),
        ],
        out_shape=[
            jax.ShapeDtypeStruct((nt, M_WIDTH), F32),
            jax.ShapeDtypeStruct((B, M_CONV - 1, M_CONV_DIM), F32),
            jax.ShapeDtypeStruct((B, np_, LANES, LANES), F32),
        ],
        scratch_shapes=[pltpu.VMEM((SUB + C, M_CONV_DIM), F32), pltpu.VMEM((np_, LANES, LANES), F32)],
        compiler_params=_cparams(("arbitrary", "arbitrary")),
        name="mamba",
    )(z, small, cw, cb, dtb, alog, dvec, ng, cbuf, h0)


def _foxprep_kernel(z_ref, ff_ref, gq_ref, gk_ref, fb_ref, m64_ref, exp_ref,
                    qb_ref, kf_ref, kb_ref, vf_ref, vb_ref, lf_ref, ccol_ref, crow_ref, carry_ref, *, tm, cum):
    lowp = qb_ref.dtype
    m64 = m64_ref[...]

    def headnorm(x, g):
        x2 = x * x
        hi = x2.astype(BF16)
        lo = (x2 - hi.astype(F32)).astype(BF16)
        ms = jnp.dot(hi, m64, preferred_element_type=F32) + jnp.dot(lo, m64, preferred_element_type=F32)
        return x * lax.rsqrt(ms + EPS) * g

    qn = headnorm(z_ref[:, 0:F_WIDTH], gq_ref[...])
    kn = headnorm(z_ref[:, F_WIDTH:2 * F_WIDTH], gk_ref[...])
    v = z_ref[:, 2 * F_WIDTH:3 * F_WIDTH]
    qb_ref[...] = (qn * (F_DH ** -0.5)).astype(lowp)
    kf_ref[...] = kn
    kb_ref[...] = kn.astype(lowp)
    vf_ref[...] = v
    vb_ref[...] = v.astype(lowp)
    lane = lax.broadcasted_iota(jnp.int32, (tm, LANES), 1)
    logf = jnp.where(lane < F_HEADS, _log_sigmoid(ff_ref[...] + fb_ref[...]), 0.0)
    lf_ref[...] = logf
    if cum:
        @pl.when(pl.program_id(1) == 0)
        def _():
            carry_ref[...] = jnp.zeros_like(carry_ref)

        cs = _cumsum_rows(logf) + carry_ref[...]
        carry_ref[...] = cs[tm - 1:tm, :]
        ccol_ref[...] = _exact_dot(cs, exp_ref[...])
        crow = _exact_rows(_eye_rows(SUB, LANES), cs)
        for p in range(F_HEADS // 2):
            crow_ref[p] = crow[2 * p:2 * p + 2, :]
    else:
        ccol_ref[...] = jnp.zeros_like(ccol_ref)
        crow_ref[...] = jnp.zeros_like(crow_ref)


def _foxprep(z, small, gq, gk, fb, m64, expand, *, B, nb, tm, cum, lowp):
    nt = B * nb * tm
    kern = functools.partial(_foxprep_kernel, tm=tm, cum=cum)
    zc = Z_FOX // 2048
    row = lambda b, i: (b * nb + i, 0)
    return pl.pallas_call(
        kern,
        grid=(B, nb),
        in_specs=[
            pl.BlockSpec((tm, 2048), lambda b, i: (b * nb + i, zc)),
            pl.BlockSpec((tm, LANES), row),
            pl.BlockSpec((1, F_WIDTH), lambda b, i: (0, 0)),
            pl.BlockSpec((1, F_WIDTH), lambda b, i: (0, 0)),
            pl.BlockSpec((1, LANES), lambda b, i: (0, 0)),
            pl.BlockSpec((F_WIDTH, F_WIDTH), lambda b, i: (0, 0)),
            pl.BlockSpec((LANES, F_WIDTH), lambda b, i: (0, 0)),
        ],
        out_specs=[
            pl.BlockSpec((tm, F_WIDTH), row),
            pl.BlockSpec((tm, F_WIDTH), row),
            pl.BlockSpec((tm, F_WIDTH), row),
            pl.BlockSpec((tm, F_WIDTH), row),
            pl.BlockSpec((tm, F_WIDTH), row),
            pl.BlockSpec((tm, LANES), row),
            pl.BlockSpec((tm, F_WIDTH), row),
            pl.BlockSpec((None, None, F_HEADS // 2, 2, tm), lambda b, i: (b, i, 0, 0, 0)),
        ],
        out_shape=[
            jax.ShapeDtypeStruct((nt, F_WIDTH), lowp),
            jax.ShapeDtypeStruct((nt, F_WIDTH), F32),
            jax.ShapeDtypeStruct((nt, F_WIDTH), lowp),
            jax.ShapeDtypeStruct((nt, F_WIDTH), F32),
            jax.ShapeDtypeStruct((nt, F_WIDTH), lowp),
            jax.ShapeDtypeStruct((nt, LANES), F32),
            jax.ShapeDtypeStruct((nt, F_WIDTH), F32),
            jax.ShapeDtypeStruct((B, nb, F_HEADS // 2, 2, tm), F32),
        ],
        scratch_shapes=[pltpu.VMEM((1, LANES), F32)],
        compiler_params=_cparams(("arbitrary", "arbitrary")),
        name="foxprep",
    )(z, small, gq, gk, fb, m64, expand)


def _foxattn_kernel(q_ref, k_ref, v_ref, cq_ref, ck_ref, fg_ref, o_ref, m_ref, l_ref, acc_ref, *, tq):
    qi = pl.program_id(2)
    q = q_ref[...]
    lane = lax.broadcasted_iota(jnp.int32, (tq, LANES), 1)
    lo = lane < F_DH
    zero = jnp.zeros_like(q)
    q2 = jnp.concatenate([jnp.where(lo, q, zero), jnp.where(lo, zero, q)], axis=0)
    cqb = cq_ref[...]
    cq2 = jnp.concatenate([cqb[:, 0:1], cqb[:, F_DH:F_DH + 1]], axis=0)
    m_ref[...] = jnp.full_like(m_ref, NEG)
    l_ref[...] = jnp.zeros_like(l_ref)
    acc_ref[...] = jnp.zeros_like(acc_ref)
    di = lax.broadcasted_iota(jnp.int32, (2 * tq, tq), 0)
    dj = lax.broadcasted_iota(jnp.int32, (2 * tq, tq), 1)
    causal = (di % tq) >= dj

    def block(j, masked):
        start = pl.multiple_of(j * tq, tq)
        k = k_ref[pl.ds(start, tq), :]
        v = v_ref[pl.ds(start, tq), :]
        s = lax.dot_general(q2, k, NT_DIMS, preferred_element_type=F32)
        ck = ck_ref[j]
        ck2 = jnp.concatenate([jnp.broadcast_to(ck[0:1, :], (tq, tq)), jnp.broadcast_to(ck[1:2, :], (tq, tq))], axis=0)
        s = s + cq2 - ck2
        if masked:
            s = jnp.where(causal, s, NEG)
        m_prev = m_ref[...]
        m_new = jnp.maximum(m_prev, jnp.max(s, axis=-1, keepdims=True))
        alpha = jnp.exp(m_prev - m_new)
        p = jnp.exp(s - m_new)
        l_ref[...] = alpha * l_ref[...] + jnp.sum(p, axis=-1, keepdims=True)
        acc_ref[...] = alpha * acc_ref[...] + jnp.dot(p.astype(BF16), v, preferred_element_type=F32)
        m_ref[...] = m_new

    def body(j, carry):
        block(j, False)
        return carry

    lax.fori_loop(0, qi, body, 0)
    block(qi, True)
    on = acc_ref[...] / l_ref[...]
    o_ref[...] = jnp.where(lo, on[0:tq, :], on[tq:2 * tq, :]) * _silu(fg_ref[...])


def _foxattn(qb, kb, vb, ccol, crow, z, *, B, L, tq):
    nq = L // tq
    npair = F_HEADS // 2
    fgc = (Z_FOX + 3 * F_WIDTH) // LANES
    kern = functools.partial(_foxattn_kernel, tq=tq)
    return pl.pallas_call(
        kern,
        grid=(B, npair, nq),
        in_specs=[
            pl.BlockSpec((tq, LANES), lambda b, p, i: (b * nq + i, p)),
            pl.BlockSpec((L, LANES), lambda b, p, i: (b, p)),
            pl.BlockSpec((L, LANES), lambda b, p, i: (b, p)),
            pl.BlockSpec((tq, LANES), lambda b, p, i: (b * nq + i, p)),
            pl.BlockSpec((None, nq, None, 2, tq), lambda b, p, i: (b, 0, p, 0, 0)),
            pl.BlockSpec((tq, LANES), lambda b, p, i: (b * nq + i, fgc + p)),
        ],
        out_specs=pl.BlockSpec((tq, LANES), lambda b, p, i: (b * nq + i, p)),
        out_shape=jax.ShapeDtypeStruct((B * L, F_WIDTH), F32),
        scratch_shapes=[pltpu.VMEM((2 * tq, 1), F32), pltpu.VMEM((2 * tq, 1), F32), pltpu.VMEM((2 * tq, LANES), F32)],
        compiler_params=_cparams(("arbitrary", "arbitrary", "arbitrary")),
        name="foxattn",
    )(qb, kb, vb, ccol, crow, z)


def _pagecum_kernel(pt_ref, lf_hbm, ck_ref, tot_ref, g_ref, sem, *, n_pages, page_off):
    b = pl.program_id(0)

    def copy(pg):
        return pltpu.make_async_copy(
            lf_hbm.at[pl.ds(pt_ref[b, pg] + page_off, 1)], g_ref.at[pl.ds(pg, 1)], sem)

    def start(pg, c):
        copy(pg).start()
        return c

    def wait(pg, c):
        copy(pg).wait()
        return c

    lax.fori_loop(0, n_pages, start, 0)
    lax.fori_loop(0, n_pages, wait, 0)

    w = g_ref[...]
    width = PAGE * F_HEADS
    lane = lax.broadcasted_iota(jnp.int32, w.shape, 1)
    s = F_HEADS
    while s < width:
        w = w + jnp.where(lane >= s, pltpu.roll(w, s, 1), 0.0)
        s *= 2
    lane1 = lax.broadcasted_iota(jnp.int32, (n_pages, LANES), 1)
    tot = jnp.where(lane1 < F_HEADS, pltpu.roll(w[:, width - LANES:width], F_HEADS, 1), 0.0)
    inc = _cumsum_rows(tot)
    off = inc - tot
    s = F_HEADS
    while s < LANES:
        off = off + pltpu.roll(off, s, 1)
        s *= 2
    cfull = w + jnp.concatenate([off] * (width // LANES), axis=1)
    src = lax.broadcasted_iota(jnp.int32, (width, width), 0)
    dst = lax.broadcasted_iota(jnp.int32, (width, width), 1)
    perm = ((src % F_HEADS) * PAGE + src // F_HEADS == dst).astype(BF16)
    cperm = _exact_dot(cfull, perm)
    for h in range(F_HEADS):
        ck_ref[h] = cperm[:, h * PAGE:(h + 1) * PAGE]
    tot_ref[...] = inc[n_pages - 1:n_pages, :]


def _pagecum(page_table, lf_cache, *, page_off):
    bd, n_pages = page_table.shape
    kern = functools.partial(_pagecum_kernel, n_pages=n_pages, page_off=page_off)
    return pl.pallas_call(
        kern,
        grid_spec=pltpu.PrefetchScalarGridSpec(
            num_scalar_prefetch=1,
            grid=(bd,),
            in_specs=[pl.BlockSpec(memory_space=pl.ANY)],
            out_specs=[
                pl.BlockSpec((None, F_HEADS, n_pages, PAGE), lambda b, pt: (b, 0, 0, 0)),
                pl.BlockSpec((None, 1, LANES), lambda b, pt: (b, 0, 0)),
            ],
            scratch_shapes=[pltpu.VMEM((n_pages, PAGE * F_HEADS), F32), pltpu.SemaphoreType.DMA],
        ),
        out_shape=[
            jax.ShapeDtypeStruct((bd, F_HEADS, n_pages, PAGE), F32),
            jax.ShapeDtypeStruct((bd, 1, LANES), F32),
        ],
        compiler_params=_cparams(("arbitrary",)),
        name="pagecum",
    )(page_table, lf_cache)


def _decode_kernel(pt_ref, q_ref, kn_ref, vn_ref, lfn_ref, tot_ref, ck_ref, fg_ref, k_hbm, v_hbm,
                   o_ref, kbuf, vbuf, sem, qbd_ref, cq_ref, cn_ref, m_ref, l_ref, acc_ref,
                   *, T, ppb, ns, page_off):
    b = pl.program_id(0)
    s = pl.program_id(1)
    t = b * ns + s
    nt = pl.num_programs(0) * ns
    R = T * F_HEADS

    def copies(bb, ss, slot):
        out = []
        for i in range(ppb):
            page = pt_ref[bb, ss * ppb + i] + page_off
            out.append(pltpu.make_async_copy(k_hbm.at[page], kbuf.at[slot, i], sem.at[0, slot]))
            out.append(pltpu.make_async_copy(v_hbm.at[page], vbuf.at[slot, i], sem.at[1, slot]))
        return out

    @pl.when(t == 0)
    def _():
        for cp in copies(b, s, 0):
            cp.start()

    @pl.when(t + 1 < nt)
    def _():
        t1 = t + 1
        for cp in copies(t1 // ns, t1 % ns, t1 % 2):
            cp.start()

    rl = lax.broadcasted_iota(jnp.int32, (SUB, F_WIDTH), 1) // F_DH
    rh = lax.broadcasted_iota(jnp.int32, (SUB, F_WIDTH), 0)
    hmask = rl == rh

    @pl.when(s == 0)
    def _():
        q = q_ref[...].astype(F32)
        qbd_ref[...] = jnp.concatenate(
            [jnp.where(hmask, jnp.broadcast_to(q[tt:tt + 1, :], (SUB, F_WIDTH)), 0.0) for tt in range(T)],
            axis=0).astype(BF16)
        cn = tot_ref[...] + _cumsum_rows(lfn_ref[...])
        cn_ref[...] = cn
        eye = (lax.broadcasted_iota(jnp.int32, (SUB, LANES), 0) == lax.broadcasted_iota(jnp.int32, (SUB, LANES), 1))
        cq_ref[...] = jnp.concatenate(
            [jnp.sum(jnp.where(eye, jnp.broadcast_to(cn[tt:tt + 1, :], (SUB, LANES)), 0.0), axis=-1, keepdims=True)
             for tt in range(T)], axis=0)
        m_ref[...] = jnp.full_like(m_ref, NEG)
        l_ref[...] = jnp.zeros_like(l_ref)
        acc_ref[...] = jnp.zeros_like(acc_ref)

    slot = t % 2
    for cp in copies(b, s, slot):
        cp.wait()

    def update(sc, v):
        m_prev = m_ref[...]
        m_new = jnp.maximum(m_prev, jnp.max(sc, axis=-1, keepdims=True))
        alpha = jnp.exp(m_prev - m_new)
        p = jnp.exp(sc - m_new)
        l_ref[...] = alpha * l_ref[...] + jnp.sum(p, axis=-1, keepdims=True)
        acc_ref[...] = alpha * acc_ref[...] + jnp.dot(p.astype(BF16), v, preferred_element_type=F32)
        m_ref[...] = m_new

    qbd = qbd_ref[...]
    cq = cq_ref[...]
    kb = kbuf[slot].reshape(ppb * PAGE, F_WIDTH).astype(BF16)
    vb = vbuf[slot].reshape(ppb * PAGE, F_WIDTH).astype(BF16)
    sc = lax.dot_general(qbd, kb, NT_DIMS, preferred_element_type=F32)
    ck = ck_ref[...]
    sc = sc + cq - jnp.concatenate([ck] * T, axis=0)
    update(sc, vb)

    @pl.when(s == ns - 1)
    def _():
        cn = cn_ref[...]
        cn_pad = jnp.concatenate([cn, jnp.zeros((PAGE - SUB, LANES), F32)], axis=0)
        cn_row = _exact_rows(_eye_rows(SUB, LANES), cn_pad)
        sn = lax.dot_general(qbd, kn_ref[...].astype(BF16), NT_DIMS, preferred_element_type=F32)
        sn = sn + cq - jnp.concatenate([cn_row] * T, axis=0)
        rt = lax.broadcasted_iota(jnp.int32, (R, PAGE), 0) // F_HEADS
        ct = lax.broadcasted_iota(jnp.int32, (R, PAGE), 1)
        sn = jnp.where((ct <= rt) & (ct < T), sn, NEG)
        update(sn, vn_ref[...].astype(BF16))
        on = acc_ref[...] / l_ref[...]
        for tt in range(T):
            blk = jnp.where(hmask, on[tt * F_HEADS:(tt + 1) * F_HEADS, :], 0.0)
            o_ref[tt:tt + 1, :] = jnp.sum(blk, axis=0, keepdims=True) * _silu(fg_ref[tt:tt + 1, :])
        o_ref[T:SUB, :] = jnp.zeros((SUB - T, F_WIDTH), F32)


def _decode(page_table, qb, kn_pad, vn_pad, lf, tot, ck, z, k_cache, v_cache, *, T, page_off, ppb):
    bd, n_pages = page_table.shape
    ns = n_pages // ppb
    R = T * F_HEADS
    fgc = (Z_FOX + 3 * F_WIDTH) // F_WIDTH
    kern = functools.partial(_decode_kernel, T=T, ppb=ppb, ns=ns, page_off=page_off)
    return pl.pallas_call(
        kern,
        grid_spec=pltpu.PrefetchScalarGridSpec(
            num_scalar_prefetch=1,
            grid=(bd, ns),
            in_specs=[
                pl.BlockSpec((TPAD, F_WIDTH), lambda b, s, pt: (b, 0)),
                pl.BlockSpec((None, PAGE, F_WIDTH), lambda b, s, pt: (b, 0, 0)),
                pl.BlockSpec((None, PAGE, F_WIDTH), lambda b, s, pt: (b, 0, 0)),
                pl.BlockSpec((TPAD, LANES), lambda b, s, pt: (b, 0)),
                pl.BlockSpec((None, 1, LANES), lambda b, s, pt: (b, 0, 0)),
                pl.BlockSpec((None, F_HEADS, ppb * PAGE), lambda b, s, pt: (b, 0, s)),
                pl.BlockSpec((TPAD, F_WIDTH), lambda b, s, pt: (b, fgc)),
                pl.BlockSpec(memory_space=pl.ANY),
                pl.BlockSpec(memory_space=pl.ANY),
            ],
            out_specs=pl.BlockSpec((TPAD, F_WIDTH), lambda b, s, pt: (b, 0)),
            scratch_shapes=[
                pltpu.VMEM((2, ppb, PAGE, F_WIDTH), F32),
                pltpu.VMEM((2, ppb, PAGE, F_WIDTH), F32),
                pltpu.SemaphoreType.DMA((2, 2)),
                pltpu.VMEM((R, F_WIDTH), BF16),
                pltpu.VMEM((R, 1), F32),
                pltpu.VMEM((SUB, LANES), F32),
                pltpu.VMEM((R, 1), F32),
                pltpu.VMEM((R, 1), F32),
                pltpu.VMEM((R, F_WIDTH), F32),
            ],
        ),
        out_shape=jax.ShapeDtypeStruct((bd * TPAD, F_WIDTH), F32),
        compiler_params=_cparams(("arbitrary", "arbitrary")),
        name="decode",
    )(page_table, qb, kn_pad, vn_pad, lf, tot, ck, z, k_cache, v_cache)


def _merge_kernel(x_ref, r_ref, f_ref, m_ref, gate_ref, wr_ref, wf_ref, wm_ref, wo_ref, y_ref):
    br = _bdot(r_ref[...], wr_ref[...])
    bf = _bdot(f_ref[...], wf_ref[...])
    bm = _bdot(m_ref[...], wm_ref[...])
    merged = (jax.nn.sigmoid(gate_ref[:, 0:D_MODEL]) * br
              + jax.nn.sigmoid(gate_ref[:, D_MODEL:2 * D_MODEL]) * bf
              + jax.nn.sigmoid(gate_ref[:, 2 * D_MODEL:3 * D_MODEL]) * bm)
    y_ref[...] = x_ref[...] + _bdot(merged, wo_ref[...])


def _merge(x, r, f, m, z, wr, wf, wm, wo):
    nt = x.shape[0]
    tm = min(nt, 512)
    row = lambda i: (i, 0)
    full = lambda i: (0, 0)
    return pl.pallas_call(
        _merge_kernel,
        grid=(nt // tm,),
        in_specs=[
            pl.BlockSpec((tm, D_MODEL), row),
            pl.BlockSpec((tm, 512), row),
            pl.BlockSpec((tm, 512), row),
            pl.BlockSpec((tm, 512), row),
            pl.BlockSpec((tm, 3 * D_MODEL), lambda i: (i, Z_GATE // (3 * D_MODEL))),
            pl.BlockSpec((512, D_MODEL), full),
            pl.BlockSpec((512, D_MODEL), full),
            pl.BlockSpec((512, D_MODEL), full),
            pl.BlockSpec((D_MODEL, D_MODEL), full),
        ],
        out_specs=pl.BlockSpec((tm, D_MODEL), row),
        out_shape=jax.ShapeDtypeStruct((nt, D_MODEL), F32),
        compiler_params=_cparams(("arbitrary",)),
        name="merge",
    )(x, r, f, m, z, wr, wf, wm, wo)


def _rope_tables(pos):
    inv = ROPE_BASE ** (-jnp.arange(0, R_DK, 2, dtype=F32) / R_DK)
    ang = pos.astype(F32)[:, None] * inv[None, :]
    cos, sin = jnp.cos(ang), jnp.sin(ang)
    cos_h = jnp.concatenate([cos, cos], axis=1)
    sin_h = jnp.concatenate([-sin, sin], axis=1)
    return jnp.tile(cos_h, (1, R_HEADS)), jnp.tile(sin_h, (1, R_HEADS))


def _pad_lanes(v, off=0, width=LANES):
    return jnp.zeros((1, width), F32).at[0, off:off + v.shape[0]].set(v)


def _layer_weights(l, norm_g, w_in, r_norm_g, f_qnorm_g, f_knorm_g, f_bias, m_conv_w, m_conv_b,
                   m_dt_bias, m_A_log, m_D, m_norm_g, w_br_r, w_br_f, w_br_m, w_out):
    w = w_in[l]
    w_main = jnp.concatenate(
        [w[:, 0:1536], w[:, 3592:4616], w[:, 4624:5136], w[:, 5136:8208], w[:, 1536:3072], w[:, 3080:3592]],
        axis=1).astype(BF16)
    zpad = jnp.zeros((D_MODEL, LANES - F_HEADS), F32)
    w_small = jnp.concatenate([w[:, 3072:3080], zpad, w[:, 4616:4624], zpad], axis=1).astype(BF16)
    return dict(
        g=norm_g[l][None, :], w_main=w_main, w_small=w_small,
        gn=r_norm_g[l][None, :],
        gq=jnp.tile(f_qnorm_g[l], F_HEADS)[None, :], gk=jnp.tile(f_knorm_g[l], F_HEADS)[None, :],
        fb=_pad_lanes(f_bias[l]),
        cw=m_conv_w[l], cb=m_conv_b[l][None, :],
        dtb=_pad_lanes(m_dt_bias[l]), alog=_pad_lanes(m_A_log[l]),
        dvec=jnp.repeat(m_D[l], M_HD)[None, :], ng=m_norm_g[l][None, :],
        wr=w_br_r[l].astype(BF16), wf=w_br_f[l].astype(BF16), wm=w_br_m[l].astype(BF16), wo=w_out[l].astype(BF16),
    )


def _consts():
    i = np.arange(F_WIDTH)
    m64 = jnp.asarray((i[:, None] // F_DH == i[None, :] // F_DH).astype(np.float32) / F_DH, dtype=BF16)
    expand = jnp.asarray((np.arange(LANES)[:, None] == i[None, :] // F_DH).astype(np.float32), dtype=BF16)
    return m64, expand


def kernel(x_prompt, x_sample, cache_fox_k, cache_fox_v, cache_fox_logf, page_table, state_ret, state_conv, state_ssm, norm_g, w_in, r_norm_g, f_qnorm_g, f_knorm_g, f_bias, m_conv_w, m_conv_b, m_dt_bias, m_A_log, m_D, m_norm_g, w_br_r, w_br_f, w_br_m, w_out):
    B, L, D = x_prompt.shape
    Bd, T, _ = x_sample.shape
    depth = w_in.shape[0]
    n_pool = cache_fox_k.shape[1]
    n_pages = page_table.shape[1]
    past = n_pages * PAGE
    C = CHUNK
    n = L // C
    tq = min(L, 256)
    ppb = min(n_pages, 16)

    m64, expand = _consts()
    cos_p, sin_p = _rope_tables(jnp.arange(L))
    cos_s, sin_s = _rope_tables(past + jnp.arange(C))
    kc = cache_fox_k.reshape(depth * n_pool, PAGE, F_WIDTH)
    vc = cache_fox_v.reshape(depth * n_pool, PAGE, F_WIDTH)
    lc = cache_fox_logf.reshape(depth * n_pool, PAGE * F_HEADS)

    hp = x_prompt.reshape(B * L, D)
    hs = jnp.pad(x_sample, ((0, 0), (0, TPAD - T), (0, 0))).reshape(Bd * TPAD, D)
    zeros_ret = jnp.zeros((B, 2, LANES, LANES), F32)
    zeros_conv = jnp.zeros((B, M_CONV - 1, M_CONV_DIM), F32)
    zeros_ssm = jnp.zeros((B, M_HEADS // 2, LANES, LANES), F32)

    def pad_chunk(a):
        w = a.shape[1]
        return jnp.pad(a.reshape(Bd, TPAD, w), ((0, 0), (0, C - TPAD), (0, 0))).reshape(Bd * C, w)

    def unpad_chunk(a):
        w = a.shape[1]
        return a.reshape(Bd, C, w)[:, :TPAD].reshape(Bd * TPAD, w)

    outs_p, outs_s = [], []
    for l in range(depth):
        W = _layer_weights(l, norm_g, w_in, r_norm_g, f_qnorm_g, f_knorm_g, f_bias, m_conv_w, m_conv_b,
                           m_dt_bias, m_A_log, m_D, m_norm_g, w_br_r, w_br_f, w_br_m, w_out)
        z, small = _inproj(hp, W["g"], W["w_main"], W["w_small"])
        r_out, ret_p = _retention(z, Z_RET // 1536, cos_p, sin_p, W["gn"], zeros_ret, B=B, n=n, C=C, T=C)
        qb, kf, kb, vf, vb, lf, ccol, crow = _foxprep(
            z, small, W["gq"], W["gk"], W["fb"], m64, expand, B=B, nb=L // tq, tm=tq, cum=True, lowp=BF16)
        f_out = _foxattn(qb, kb, vb, ccol, crow, z, B=B, L=L, tq=tq)
        m_out, conv_p, ssm_p = _mamba(z, Z_MAM // 1536, small, 1, W["cw"], W["cb"], W["dtb"], W["alog"],
                                      W["dvec"], W["ng"], zeros_conv, zeros_ssm, B=B, n=n, C=C, T=C)
        hp = _merge(hp, r_out, f_out, m_out, z, W["wr"], W["wf"], W["wm"], W["wo"])
        outs_p.append((kf, vf, lf[:, :F_HEADS], ret_p, conv_p, ssm_p))

        zs, smalls = _inproj(hs, W["g"], W["w_main"], W["w_small"])
        r_pad, ret_s = _retention(pad_chunk(zs[:, Z_RET:Z_RET + 1536]), 0, cos_s, sin_s, W["gn"],
                                  state_ret[l].reshape(Bd, 2, LANES, LANES), B=Bd, n=1, C=C, T=T)
        qb, kf, kb, vf, vb, lf, _, _ = _foxprep(
            zs, smalls, W["gq"], W["gk"], W["fb"], m64, expand, B=Bd, nb=1, tm=TPAD, cum=False, lowp=F32)
        ck, tot = _pagecum(page_table, lc, page_off=l * n_pool)
        f_out = _decode(page_table, qb, pad_chunk(kb).reshape(Bd, C, F_WIDTH), pad_chunk(vb).reshape(Bd, C, F_WIDTH),
                        lf, tot, ck.reshape(Bd, F_HEADS, n_pages * PAGE), zs, kc, vc,
                        T=T, page_off=l * n_pool, ppb=ppb)
        m_pad, conv_s, ssm_s = _mamba(pad_chunk(zs[:, Z_MAM:Z_MAM + 1536]), 0, pad_chunk(smalls[:, LANES:]), 0,
                                      W["cw"], W["cb"], W["dtb"], W["alog"], W["dvec"], W["ng"],
                                      state_conv[l], state_ssm[l].reshape(Bd, M_HEADS // 2, LANES, LANES),
                                      B=Bd, n=1, C=C, T=T)
        hs = _merge(hs, unpad_chunk(r_pad), f_out, unpad_chunk(m_pad), zs, W["wr"], W["wf"], W["wm"], W["wo"])
        sel = lambda a: a.reshape(Bd, TPAD, -1)[:, :T]
        outs_s.append((sel(kf), sel(vf), sel(lf)[:, :, :F_HEADS], ret_s, conv_s, ssm_s))

    stk = lambda outs, i: jnp.stack([o[i] for o in outs])
    return (
        hp.reshape(B, L, D),
        hs.reshape(Bd, TPAD, D)[:, :T],
        stk(outs_p, 0).reshape(depth, B, L, F_HEADS, F_DH),
        stk(outs_p, 1).reshape(depth, B, L, F_HEADS, F_DH),
        stk(outs_p, 2).reshape(depth, B, L, F_HEADS),
        stk(outs_p, 3).reshape(depth, B, R_HEADS, R_DK, R_DV),
        stk(outs_p, 4),
        stk(outs_p, 5).reshape(depth, B, M_HEADS, M_HD, M_STATE),
        stk(outs_s, 0).reshape(depth, Bd, T, F_HEADS, F_DH),
        stk(outs_s, 1).reshape(depth, Bd, T, F_HEADS, F_DH),
        stk(outs_s, 2),
        stk(outs_s, 3).reshape(depth, Bd, R_HEADS, R_DK, R_DV),
        stk(outs_s, 4),
        stk(outs_s, 5).reshape(depth, Bd, M_HEADS, M_HD, M_STATE),
    )
```

```python
import functools
import math

import numpy as np
import jax
import jax.numpy as jnp
from jax import lax
from jax.experimental import pallas as pl
from jax.experimental.pallas import tpu as pltpu

F32 = jnp.float32
BF16 = jnp.bfloat16

D_MODEL = 1024
PAGE = 128
R_HEADS, R_DK, R_DV = 4, 64, 128
F_HEADS, F_DH = 8, 64
F_WIDTH = F_HEADS * F_DH
M_HEADS, M_HD, M_GROUPS, M_STATE, M_CONV = 8, 64, 2, 128, 4
M_WIDTH = M_HEADS * M_HD
M_CONV_DIM = M_WIDTH + 2 * M_GROUPS * M_STATE
CHUNK = 128
ROPE_BASE = 10000.0
EPS = 1e-6
NEG = -1e30

Z_RET, Z_MAM, Z_GATE, Z_FOX = 0, 1536, 3072, 6144
Z_WIDTH = 8192
SMALL_W = 256
LANES = 128
SUB = 8
TPAD = 8

VMEM_LIMIT = 56 * 1024 * 1024

NT_DIMS = (((1,), (1,)), ((), ()))
TN_DIMS = (((0,), (0,)), ((), ()))

LOG_G = [float(np.log1p(-np.exp2(-5.0 - h))) for h in range(R_HEADS)]


def _cparams(sem):
    return pltpu.CompilerParams(dimension_semantics=sem, vmem_limit_bytes=VMEM_LIMIT)


def _silu(x):
    return x * jax.nn.sigmoid(x)


def _softplus(x):
    return jnp.maximum(x, 0.0) + jnp.log1p(jnp.exp(-jnp.abs(x)))


def _log_sigmoid(x):
    return jnp.minimum(x, 0.0) - jnp.log1p(jnp.exp(-jnp.abs(x)))


def _bdot(a, b):
    return jnp.dot(a.astype(BF16), b.astype(BF16), preferred_element_type=F32)


def _bdot_nt(a, b):
    return lax.dot_general(a.astype(BF16), b.astype(BF16), NT_DIMS, preferred_element_type=F32)


def _bdot_tn(a, b):
    return lax.dot_general(a.astype(BF16), b.astype(BF16), TN_DIMS, preferred_element_type=F32)


def _split3(x):
    h1 = x.astype(BF16)
    r1 = x - h1.astype(F32)
    h2 = r1.astype(BF16)
    h3 = (r1 - h2.astype(F32)).astype(BF16)
    return h1, h2, h3


def _exact_dot(x, sel):
    h1, h2, h3 = _split3(x)
    d = lambda a: jnp.dot(a, sel, preferred_element_type=F32)
    return d(h1) + d(h2) + d(h3)


def _exact_rows(sel, x):
    h1, h2, h3 = _split3(x)
    d = lambda a: lax.dot_general(sel, a, NT_DIMS, preferred_element_type=F32)
    return d(h1) + d(h2) + d(h3)


def _cumsum_rows(x):
    n = x.shape[0]
    i = lax.broadcasted_iota(jnp.int32, x.shape, 0)
    s = 1
    while s < n:
        x = x + jnp.where(i >= s, pltpu.roll(x, s, 0), 0.0)
        s *= 2
    return x


def _eye_rows(r, k):
    return (lax.broadcasted_iota(jnp.int32, (r, k), 0) == lax.broadcasted_iota(jnp.int32, (r, k), 1)).astype(BF16)


def _inproj_kernel(x_ref, g_ref, w_ref, ws_ref, z_ref, s_ref, h_ref):
    @pl.when(pl.program_id(1) == 0)
    def _():
        x = x_ref[...]
        r = x * lax.rsqrt(jnp.mean(x * x, axis=-1, keepdims=True) + EPS)
        h = (r * g_ref[...]).astype(BF16)
        h_ref[...] = h
        s_ref[...] = jnp.dot(h, ws_ref[...], preferred_element_type=F32)

    z_ref[...] = jnp.dot(h_ref[...], w_ref[...], preferred_element_type=F32)


def _inproj(x, g, w_main, w_small):
    nt = x.shape[0]
    tm = min(nt, 1024)
    tn = 1024
    return pl.pallas_call(
        _inproj_kernel,
        grid=(nt // tm, Z_WIDTH // tn),
        in_specs=[
            pl.BlockSpec((tm, D_MODEL), lambda i, j: (i, 0)),
            pl.BlockSpec((1, D_MODEL), lambda i, j: (0, 0)),
            pl.BlockSpec((D_MODEL, tn), lambda i, j: (0, j)),
            pl.BlockSpec((D_MODEL, SMALL_W), lambda i, j: (0, 0)),
        ],
        out_specs=[
            pl.BlockSpec((tm, tn), lambda i, j: (i, j)),
            pl.BlockSpec((tm, SMALL_W), lambda i, j: (i, 0)),
        ],
        out_shape=[jax.ShapeDtypeStruct((nt, Z_WIDTH), F32), jax.ShapeDtypeStruct((nt, SMALL_W), F32)],
        scratch_shapes=[pltpu.VMEM((tm, D_MODEL), BF16)],
        compiler_params=_cparams(("arbitrary", "arbitrary")),
        name="inproj",
    )(x, g, w_main, w_small)


def _ret_kernel(z_ref, cos_ref, sin_ref, gn_ref, s0_ref, o_ref, sl_ref, st_ref, *, C, T):
    c = pl.program_id(1)

    @pl.when(c == 0)
    def _():
        st_ref[...] = s0_ref[...]

    lane2 = lax.broadcasted_iota(jnp.int32, (C, 2 * LANES), 1)
    first = (lane2 % R_DK) < (R_DK // 2)
    cos = cos_ref[...]
    sin = sin_ref[...]

    def rope(x):
        xr = jnp.where(first, pltpu.roll(x, 2 * LANES - R_DK // 2, 1), pltpu.roll(x, R_DK // 2, 1))
        return x * cos + xr * sin

    q = rope(z_ref[:, 0:256]) * (R_DK ** -0.5)
    k = rope(z_ref[:, 256:512])

    ri = lax.broadcasted_iota(jnp.int32, (C, 1), 0)
    rif = ri.astype(F32)
    valid = ri < T
    di = lax.broadcasted_iota(jnp.int32, (C, C), 0)
    dj = lax.broadcasted_iota(jnp.int32, (C, C), 1)
    dd = jnp.maximum(di - dj, 0).astype(F32)
    causal = di >= dj
    lane = lax.broadcasted_iota(jnp.int32, (C, LANES), 1)
    lo = lane < R_DK
    row = lax.broadcasted_iota(jnp.int32, (LANES, LANES), 0)
    rlo = row < R_DK

    for p in range(R_HEADS // 2):
        qp = q[:, p * LANES:(p + 1) * LANES]
        kp = jnp.where(valid, k[:, p * LANES:(p + 1) * LANES], 0.0)
        sp = st_ref[p]
        ds = []
        for e in range(2):
            h = 2 * p + e
            lg = LOG_G[h]
            qm = jnp.where(lo if e == 0 else jnp.logical_not(lo), qp, 0.0)
            decay = jnp.where(causal, jnp.exp(lg * dd), 0.0)
            wq = jnp.exp(lg * (rif + 1.0))
            wk = jnp.exp(lg * (float(T - 1) - rif))
            vh = z_ref[:, 512 + h * LANES:512 + (h + 1) * LANES]
            s = _bdot_nt(qm, kp) * decay
            o = _bdot(s, vh) + _bdot(qm * wq, sp)
            mu = jnp.mean(o, axis=-1, keepdims=True)
            oc = o - mu
            var = jnp.mean(oc * oc, axis=-1, keepdims=True)
            gate = z_ref[:, 1024 + h * LANES:1024 + (h + 1) * LANES]
            o_ref[:, h * LANES:(h + 1) * LANES] = (
                oc * lax.rsqrt(var + EPS) * gn_ref[:, h * LANES:(h + 1) * LANES] * _silu(gate))
            ds.append(_bdot_tn(kp * wk, vh))
        gdec = jnp.where(rlo, math.exp(LOG_G[2 * p] * T), math.exp(LOG_G[2 * p + 1] * T))
        st_ref[p] = gdec * sp + jnp.where(rlo, ds[0], ds[1])

    @pl.when(c == pl.num_programs(1) - 1)
    def _():
        sl_ref[...] = st_ref[...]


def _retention(z, zcol, cos, sin, gn, s0, *, B, n, C, T):
    nt = B * n * C
    kern = functools.partial(_ret_kernel, C=C, T=T)
    return pl.pallas_call(
        kern,
        grid=(B, n),
        in_specs=[
            pl.BlockSpec((C, 1536), lambda b, c: (b * n + c, zcol)),
            pl.BlockSpec((C, 256), lambda b, c: (c, 0)),
            pl.BlockSpec((C, 256), lambda b, c: (c, 0)),
            pl.BlockSpec((1, 512), lambda b, c: (0, 0)),
            pl.BlockSpec((None, 2, LANES, LANES), lambda b, c: (b, 0, 0, 0)),
        ],
        out_specs=[
            pl.BlockSpec((C, 512), lambda b, c: (b * n + c, 0)),
            pl.BlockSpec((None, 2, LANES, LANES), lambda b, c: (b, 0, 0, 0)),
        ],
        out_shape=[jax.ShapeDtypeStruct((nt, 512), F32), jax.ShapeDtypeStruct((B, 2, LANES, LANES), F32)],
        scratch_shapes=[pltpu.VMEM((2, LANES, LANES), F32)],
        compiler_params=_cparams(("arbitrary", "arbitrary")),
        name="retention",
    )(z, cos, sin, gn, s0)


def _mamba_kernel(z_ref, dt_ref, cw_ref, cb_ref, dtb_ref, alog_ref, dvec_ref, ng_ref, cbuf_ref, h0_ref,
                  o_ref, cnew_ref, hl_ref, xp_ref, st_ref, *, C, T):
    c = pl.program_id(1)
    K = M_CONV

    @pl.when(c == 0)
    def _():
        xp_ref[SUB - (K - 1):SUB, :] = cbuf_ref[...]
        st_ref[...] = h0_ref[...]

    xp_ref[SUB:SUB + C, :] = z_ref[:, 0:M_CONV_DIM]
    conv = cb_ref[...]
    for j in range(K):
        conv = conv + xp_ref[SUB - (K - 1) + j:SUB - (K - 1) + j + C, :] * cw_ref[j:j + 1, :]
    tail = xp_ref[SUB + T - (K - 1):SUB + T, :]
    xp_ref[SUB - (K - 1):SUB, :] = tail
    xbc = _silu(conv)

    lane = lax.broadcasted_iota(jnp.int32, (C, LANES), 1)
    ri = lax.broadcasted_iota(jnp.int32, (C, LANES), 0)
    dt = _softplus(dt_ref[...] + dtb_ref[...])
    dt = jnp.where((lane < M_HEADS) & (ri < T), dt, 0.0)
    a = dt * (-jnp.exp(alog_ref[...]))
    a_cs = _cumsum_rows(a)
    eye8 = _eye_rows(SUB, LANES)
    a_row = _exact_rows(eye8, a_cs)
    dt_row = _exact_rows(eye8, dt)
    a_last = a_cs[C - 1:C, :]
    w_end = jnp.exp(a_last - a_cs) * dt
    ea = jnp.exp(a_cs)
    e_last = jnp.exp(a_last)

    di = lax.broadcasted_iota(jnp.int32, (C, C), 0)
    dj = lax.broadcasted_iota(jnp.int32, (C, C), 1)
    causal = di >= dj
    lo = lane < M_HD
    rlo = lax.broadcasted_iota(jnp.int32, (LANES, LANES), 0) < M_HD

    ys = []
    heads_per_group = M_HEADS // M_GROUPS
    for g in range(M_GROUPS):
        bg = xbc[:, M_WIDTH + g * M_STATE:M_WIDTH + (g + 1) * M_STATE]
        cg = xbc[:, M_WIDTH + (M_GROUPS + g) * M_STATE:M_WIDTH + (M_GROUPS + g + 1) * M_STATE]
        cbm = _bdot_nt(cg, bg)
        for pp in range(heads_per_group // 2):
            p = g * (heads_per_group // 2) + pp
            xpair = xbc[:, p * LANES:(p + 1) * LANES]
            yh = []
            for e in range(2):
                h = 2 * p + e
                seg = a_cs[:, h:h + 1] - a_row[h:h + 1, :]
                lm = jnp.exp(jnp.where(causal, seg, NEG))
                m = cbm * lm * dt_row[h:h + 1, :]
                yh.append(_bdot(m, xpair))
            sp = st_ref[p]
            ycross = _bdot_nt(cg, sp) * jnp.where(lo, ea[:, 2 * p:2 * p + 1], ea[:, 2 * p + 1:2 * p + 2])
            y = jnp.where(lo, yh[0], yh[1]) + ycross + dvec_ref[:, p * LANES:(p + 1) * LANES] * xpair
            ys.append(y)
            xs = xpair * jnp.where(lo, w_end[:, 2 * p:2 * p + 1], w_end[:, 2 * p + 1:2 * p + 2])
            dstate = _bdot_tn(xs, bg)
            dec = jnp.where(rlo, e_last[:, 2 * p:2 * p + 1], e_last[:, 2 * p + 1:2 * p + 2])
            st_ref[p] = dec * sp + dstate

    my = jnp.concatenate(ys, axis=1)
    gated = my * _silu(z_ref[:, M_CONV_DIM:M_CONV_DIM + M_WIDTH])
    o_ref[...] = gated * lax.rsqrt(jnp.mean(gated * gated, axis=-1, keepdims=True) + EPS) * ng_ref[...]

    @pl.when(c == pl.num_programs(1) - 1)
    def _():
        cnew_ref[...] = xp_ref[SUB - (K - 1):SUB, :]
        hl_ref[...] = st_ref[...]


def _mamba(z, zcol, small, scol, cw, cb, dtb, alog, dvec, ng, cbuf, h0, *, B, n, C, T):
    nt = B * n * C
    kern = functools.partial(_mamba_kernel, C=C, T=T)
    np_ = M_HEADS // 2
    return pl.pallas_call(
        kern,
        grid=(B, n),
        in_specs=[
            pl.BlockSpec((C, 1536), lambda b, c: (b * n + c, zcol)),
            pl.BlockSpec((C, LANES), lambda b, c: (b * n + c, scol)),
            pl.BlockSpec((M_CONV, M_CONV_DIM), lambda b, c: (0, 0)),
            pl.BlockSpec((1, M_CONV_DIM), lambda b, c: (0, 0)),
            pl.BlockSpec((1, LANES), lambda b, c: (0, 0)),
            pl.BlockSpec((1, LANES), lambda b, c: (0, 0)),
            pl.BlockSpec((1, M_WIDTH), lambda b, c: (0, 0)),
            pl.BlockSpec((1, M_WIDTH), lambda b, c: (0, 0)),
            pl.BlockSpec((None, M_CONV - 1, M_CONV_DIM), lambda b, c: (b, 0, 0)),
            pl.BlockSpec((None, np_, LANES, LANES), lambda b, c: (b, 0, 0, 0)),
        ],
        out_specs=[
            pl.BlockSpec((C, M_WIDTH), lambda b, c: (b * n + c, 0)),
            pl.BlockSpec((None, M_CONV - 1, M_CONV_DIM), lambda b, c: (b, 0, 0)),
            pl.BlockSpec((None, np_, LANES, LANES), lambda b, c: (b, 0, 0, 0)),
        ],
        out_shape=[
            jax.ShapeDtypeStruct((nt, M_WIDTH), F32),
            jax.ShapeDtypeStruct((B, M_CONV - 1, M_CONV_DIM), F32),
            jax.ShapeDtypeStruct((B, np_, LANES, LANES), F32),
        ],
        scratch_shapes=[pltpu.VMEM((SUB + C, M_CONV_DIM), F32), pltpu.VMEM((np_, LANES, LANES), F32)],
        compiler_params=_cparams(("arbitrary", "arbitrary")),
        name="mamba",
    )(z, small, cw, cb, dtb, alog, dvec, ng, cbuf, h0)


def _fox_qkv(z_ref, ff_ref, gq_ref, gk_ref, fb_ref, m64_ref):
    m64 = m64_ref[...]

    def headnorm(x, g):
        x2 = x * x
        hi = x2.astype(BF16)
        lo = (x2 - hi.astype(F32)).astype(BF16)
        ms = jnp.dot(hi, m64, preferred_element_type=F32) + jnp.dot(lo, m64, preferred_element_type=F32)
        return x * lax.rsqrt(ms + EPS) * g

    qs = headnorm(z_ref[:, 0:F_WIDTH], gq_ref[...]) * (F_DH ** -0.5)
    kn = headnorm(z_ref[:, F_WIDTH:2 * F_WIDTH], gk_ref[...])
    v = z_ref[:, 2 * F_WIDTH:3 * F_WIDTH]
    lane = lax.broadcasted_iota(jnp.int32, ff_ref.shape, 1)
    logf = jnp.where(lane < F_HEADS, _log_sigmoid(ff_ref[...] + fb_ref[...]), 0.0)
    return qs, kn, v, logf


def _foxprep_prompt_kernel(z_ref, ff_ref, gq_ref, gk_ref, fb_ref, m64_ref, exp_ref,
                           qt_ref, kf_ref, kb_ref, vf_ref, vt_ref, lf_ref, ccol_ref, crow_ref, carry_ref, *, tm):
    qs, kn, v, logf = _fox_qkv(z_ref, ff_ref, gq_ref, gk_ref, fb_ref, m64_ref)
    kf_ref[...] = kn
    kb_ref[...] = kn.astype(BF16)
    vf_ref[...] = v
    lf_ref[...] = logf
    for p in range(F_HEADS // 2):
        qt_ref[p] = jnp.transpose(qs[:, p * LANES:(p + 1) * LANES]).astype(BF16)
        vt_ref[p] = jnp.transpose(v[:, p * LANES:(p + 1) * LANES]).astype(BF16)

    @pl.when(pl.program_id(1) == 0)
    def _():
        carry_ref[...] = jnp.zeros_like(carry_ref)

    cs = _cumsum_rows(logf) + carry_ref[...]
    carry_ref[...] = cs[tm - 1:tm, :]
    ccol_ref[...] = _exact_dot(cs, exp_ref[...])
    crow = _exact_rows(_eye_rows(SUB, LANES), cs)
    for p in range(F_HEADS // 2):
        crow_ref[p] = crow[2 * p:2 * p + 2, :]


def _foxprep_prompt(z, small, gq, gk, fb, m64, expand, *, B, nb, tm):
    nt = B * nb * tm
    kern = functools.partial(_foxprep_prompt_kernel, tm=tm)
    zc = Z_FOX // 2048
    npair = F_HEADS // 2
    row = lambda b, i: (b * nb + i, 0)
    const = lambda b, i: (0, 0)
    return pl.pallas_call(
        kern,
        grid=(B, nb),
        in_specs=[
            pl.BlockSpec((tm, 2048), lambda b, i: (b * nb + i, zc)),
            pl.BlockSpec((tm, LANES), row),
            pl.BlockSpec((1, F_WIDTH), const),
            pl.BlockSpec((1, F_WIDTH), const),
            pl.BlockSpec((1, LANES), const),
            pl.BlockSpec((F_WIDTH, F_WIDTH), const),
            pl.BlockSpec((LANES, F_HEADS * LANES), const),
        ],
        out_specs=[
            pl.BlockSpec((None, npair, None, LANES, tm), lambda b, i: (b, 0, i, 0, 0)),
            pl.BlockSpec((tm, F_WIDTH), row),
            pl.BlockSpec((tm, F_WIDTH), row),
            pl.BlockSpec((tm, F_WIDTH), row),
            pl.BlockSpec((None, npair, None, LANES, tm), lambda b, i: (b, 0, i, 0, 0)),
            pl.BlockSpec((tm, LANES), row),
            pl.BlockSpec((tm, F_HEADS * LANES), row),
            pl.BlockSpec((None, None, npair, 2, tm), lambda b, i: (b, i, 0, 0, 0)),
        ],
        out_shape=[
            jax.ShapeDtypeStruct((B, npair, nb, LANES, tm), BF16),
            jax.ShapeDtypeStruct((nt, F_WIDTH), F32),
            jax.ShapeDtypeStruct((nt, F_WIDTH), BF16),
            jax.ShapeDtypeStruct((nt, F_WIDTH), F32),
            jax.ShapeDtypeStruct((B, npair, nb, LANES, tm), BF16),
            jax.ShapeDtypeStruct((nt, LANES), F32),
            jax.ShapeDtypeStruct((nt, F_HEADS * LANES), F32),
            jax.ShapeDtypeStruct((B, nb, npair, 2, tm), F32),
        ],
        scratch_shapes=[pltpu.VMEM((1, LANES), F32)],
        compiler_params=_cparams(("arbitrary", "arbitrary")),
        name="foxprep_prompt",
    )(z, small, gq, gk, fb, m64, expand)


def _foxprep_sample_kernel(z_ref, ff_ref, gq_ref, gk_ref, fb_ref, m64_ref, qs_ref, kf_ref, vf_ref, lf_ref):
    qs, kn, v, logf = _fox_qkv(z_ref, ff_ref, gq_ref, gk_ref, fb_ref, m64_ref)
    qs_ref[...] = qs
    kf_ref[...] = kn
    vf_ref[...] = v
    lf_ref[...] = logf


def _foxprep_sample(z, small, gq, gk, fb, m64):
    nt = z.shape[0]
    zc = Z_FOX // 2048
    row = lambda i: (i, 0)
    const = lambda i: (0, 0)
    return pl.pallas_call(
        _foxprep_sample_kernel,
        grid=(1,),
        in_specs=[
            pl.BlockSpec((nt, 2048), lambda i: (i, zc)),
            pl.BlockSpec((nt, LANES), row),
            pl.BlockSpec((1, F_WIDTH), const),
            pl.BlockSpec((1, F_WIDTH), const),
            pl.BlockSpec((1, LANES), const),
            pl.BlockSpec((F_WIDTH, F_WIDTH), const),
        ],
        out_specs=[
            pl.BlockSpec((nt, F_WIDTH), row),
            pl.BlockSpec((nt, F_WIDTH), row),
            pl.BlockSpec((nt, F_WIDTH), row),
            pl.BlockSpec((nt, LANES), row),
        ],
        out_shape=[
            jax.ShapeDtypeStruct((nt, F_WIDTH), F32),
            jax.ShapeDtypeStruct((nt, F_WIDTH), F32),
            jax.ShapeDtypeStruct((nt, F_WIDTH), F32),
            jax.ShapeDtypeStruct((nt, LANES), F32),
        ],
        compiler_params=_cparams(("arbitrary",)),
        name="foxprep_sample",
    )(z, small, gq, gk, fb, m64)


def _foxattn_kernel(qt_ref, k_ref, vt_ref, cq_ref, cke_ref, cko_ref, fg_ref, o_ref, m_ref, l_ref, acc_ref, *, tq):
    qi = pl.program_id(2)
    qt = qt_ref[...]
    rlo = lax.broadcasted_iota(jnp.int32, (LANES, tq), 0) < F_DH
    zero = jnp.zeros_like(qt)
    q2t = jnp.concatenate([jnp.where(rlo, qt, zero), jnp.where(rlo, zero, qt)], axis=1)
    cq2 = jnp.concatenate([cq_ref[0:1, :], cq_ref[1:2, :]], axis=1)
    m_ref[...] = jnp.full_like(m_ref, NEG)
    l_ref[...] = jnp.zeros_like(l_ref)
    acc_ref[...] = jnp.zeros_like(acc_ref)
    kpos = lax.broadcasted_iota(jnp.int32, (tq, 2 * tq), 0)
    qpos = lax.broadcasted_iota(jnp.int32, (tq, 2 * tq), 1) % tq
    causal = kpos <= qpos
    rep = tq // LANES

    def block(j, masked):
        start = pl.multiple_of(j * tq, tq)
        k = k_ref[pl.ds(start, tq), :]
        st = jnp.dot(k, q2t, preferred_element_type=F32)
        cke = cke_ref[pl.ds(start, tq), :]
        cko = cko_ref[pl.ds(start, tq), :]
        u = st - jnp.concatenate([cke] * rep + [cko] * rep, axis=1)
        if masked:
            u = jnp.where(causal, u, NEG)
        m_prev = m_ref[...]
        m_new = jnp.maximum(m_prev, jnp.max(u, axis=0, keepdims=True) + cq2)
        alpha = jnp.exp(m_prev - m_new)
        p = jnp.exp(u + (cq2 - m_new))
        l_ref[...] = alpha * l_ref[...] + jnp.sum(p, axis=0, keepdims=True)
        acc_ref[...] = alpha * acc_ref[...] + jnp.dot(vt_ref[j], p.astype(BF16), preferred_element_type=F32)
        m_ref[...] = m_new

    def body(j, carry):
        block(j, False)
        return carry

    lax.fori_loop(0, qi, body, 0)
    block(qi, True)
    on = acc_ref[...] / l_ref[...]
    ot = jnp.concatenate([on[0:F_DH, 0:tq], on[F_DH:2 * F_DH, tq:2 * tq]], axis=0)
    o_ref[...] = jnp.transpose(ot) * _silu(fg_ref[...])


def _foxattn(qt, kb, vt, ccol, crow, z, *, B, L, tq):
    nq = L // tq
    npair = F_HEADS // 2
    fgc = (Z_FOX + 3 * F_WIDTH) // LANES
    kern = functools.partial(_foxattn_kernel, tq=tq)
    return pl.pallas_call(
        kern,
        grid=(B, npair, nq),
        in_specs=[
            pl.BlockSpec((None, None, None, LANES, tq), lambda b, p, i: (b, p, i, 0, 0)),
            pl.BlockSpec((L, LANES), lambda b, p, i: (b, p)),
            pl.BlockSpec((None, None, nq, LANES, tq), lambda b, p, i: (b, p, 0, 0, 0)),
            pl.BlockSpec((None, None, None, 2, tq), lambda b, p, i: (b, i, p, 0, 0)),
            pl.BlockSpec((L, LANES), lambda b, p, i: (b, 2 * p)),
            pl.BlockSpec((L, LANES), lambda b, p, i: (b, 2 * p + 1)),
            pl.BlockSpec((tq, LANES), lambda b, p, i: (b * nq + i, fgc + p)),
        ],
        out_specs=pl.BlockSpec((tq, LANES), lambda b, p, i: (b * nq + i, p)),
        out_shape=jax.ShapeDtypeStruct((B * L, F_WIDTH), F32),
        scratch_shapes=[pltpu.VMEM((1, 2 * tq), F32), pltpu.VMEM((1, 2 * tq), F32), pltpu.VMEM((LANES, 2 * tq), F32)],
        compiler_params=_cparams(("arbitrary", "arbitrary", "arbitrary")),
        name="foxattn",
    )(qt, kb, vt, crow, ccol, ccol, z)


def _pagecum_kernel(pt_ref, lf_hbm, ck_ref, tot_ref, g_ref, sem, *, n_pages, page_off):
    b = pl.program_id(0)

    def copy(pg):
        return pltpu.make_async_copy(lf_hbm.at[pt_ref[b, pg] + page_off], g_ref.at[pg], sem)

    def start(pg, c):
        copy(pg).start()
        return c

    def wait(pg, c):
        copy(pg).wait()
        return c

    lax.fori_loop(0, n_pages, start, 0)
    lax.fori_loop(0, n_pages, wait, 0)

    rows = n_pages * F_HEADS
    w = g_ref[...].reshape(rows, PAGE)
    lane = lax.broadcasted_iota(jnp.int32, (rows, PAGE), 1)
    row = lax.broadcasted_iota(jnp.int32, (rows, PAGE), 0)
    s = 1
    while s < PAGE:
        w = w + jnp.where(lane >= s, pltpu.roll(w, s, 1), 0.0)
        s *= 2
    tot = jnp.broadcast_to(w[:, PAGE - 1:PAGE], (rows, PAGE))
    inc = tot
    s = F_HEADS
    while s < rows:
        inc = inc + jnp.where(row >= s, pltpu.roll(inc, s, 0), 0.0)
        s *= 2
    ck_ref[...] = (w + (inc - tot)).reshape(n_pages, F_HEADS, PAGE)
    last = inc[rows - F_HEADS:rows, :]
    eye = (lax.broadcasted_iota(jnp.int32, (F_HEADS, LANES), 0) == lax.broadcasted_iota(jnp.int32, (F_HEADS, LANES), 1))
    tot_ref[...] = jnp.sum(jnp.where(eye, last, 0.0), axis=0, keepdims=True)


def _pagecum(page_table, lf_cache, *, page_off):
    bd, n_pages = page_table.shape
    kern = functools.partial(_pagecum_kernel, n_pages=n_pages, page_off=page_off)
    return pl.pallas_call(
        kern,
        grid_spec=pltpu.PrefetchScalarGridSpec(
            num_scalar_prefetch=1,
            grid=(bd,),
            in_specs=[pl.BlockSpec(memory_space=pl.ANY)],
            out_specs=[
                pl.BlockSpec((None, n_pages, F_HEADS, PAGE), lambda b, pt: (b, 0, 0, 0)),
                pl.BlockSpec((None, 1, LANES), lambda b, pt: (b, 0, 0)),
            ],
            scratch_shapes=[pltpu.VMEM((n_pages, F_HEADS, PAGE), F32), pltpu.SemaphoreType.DMA],
        ),
        out_shape=[
            jax.ShapeDtypeStruct((bd, n_pages, F_HEADS, PAGE), F32),
            jax.ShapeDtypeStruct((bd, 1, LANES), F32),
        ],
        compiler_params=_cparams(("arbitrary",)),
        name="pagecum",
    )(page_table, lf_cache)


def _decode_kernel(pt_ref, q_ref, kn_ref, vn_ref, lfn_ref, tot_ref, ck_ref, fg_ref, k_hbm, v_hbm,
                   o_ref, kbuf, vbuf, sem, qbd_ref, cq_ref, cn_ref, m_ref, l_ref, acc_ref,
                   *, T, ppb, ns, page_off):
    b = pl.program_id(0)
    s = pl.program_id(1)
    t = b * ns + s
    nt = pl.num_programs(0) * ns
    R = T * F_HEADS

    def copies(bb, ss, slot):
        out = []
        for i in range(ppb):
            page = pt_ref[bb, ss * ppb + i] + page_off
            out.append(pltpu.make_async_copy(k_hbm.at[page], kbuf.at[slot, i], sem.at[0, slot]))
            out.append(pltpu.make_async_copy(v_hbm.at[page], vbuf.at[slot, i], sem.at[1, slot]))
        return out

    @pl.when(t == 0)
    def _():
        for cp in copies(b, s, 0):
            cp.start()

    @pl.when(t + 1 < nt)
    def _():
        t1 = t + 1
        for cp in copies(t1 // ns, t1 % ns, t1 % 2):
            cp.start()

    rl = lax.broadcasted_iota(jnp.int32, (SUB, F_WIDTH), 1) // F_DH
    rh = lax.broadcasted_iota(jnp.int32, (SUB, F_WIDTH), 0)
    hmask = rl == rh

    @pl.when(s == 0)
    def _():
        q = q_ref[...].astype(F32)
        qbd_ref[...] = jnp.concatenate(
            [jnp.where(hmask, jnp.broadcast_to(q[tt:tt + 1, :], (SUB, F_WIDTH)), 0.0) for tt in range(T)],
            axis=0).astype(BF16)
        cn = tot_ref[...] + _cumsum_rows(lfn_ref[...])
        cn_ref[...] = cn
        eye = (lax.broadcasted_iota(jnp.int32, (SUB, LANES), 0) == lax.broadcasted_iota(jnp.int32, (SUB, LANES), 1))
        cq_ref[...] = jnp.concatenate(
            [jnp.sum(jnp.where(eye, jnp.broadcast_to(cn[tt:tt + 1, :], (SUB, LANES)), 0.0), axis=-1, keepdims=True)
             for tt in range(T)], axis=0)
        m_ref[...] = jnp.full_like(m_ref, NEG)
        l_ref[...] = jnp.zeros_like(l_ref)
        acc_ref[...] = jnp.zeros_like(acc_ref)

    slot = t % 2
    for cp in copies(b, s, slot):
        cp.wait()

    def update(sc, v):
        m_prev = m_ref[...]
        m_new = jnp.maximum(m_prev, jnp.max(sc, axis=-1, keepdims=True))
        alpha = jnp.exp(m_prev - m_new)
        p = jnp.exp(sc - m_new)
        l_ref[...] = alpha * l_ref[...] + jnp.sum(p, axis=-1, keepdims=True)
        acc_ref[...] = alpha * acc_ref[...] + jnp.dot(p.astype(BF16), v, preferred_element_type=F32)
        m_ref[...] = m_new

    qbd = qbd_ref[...]
    cq = cq_ref[...]
    sc = jnp.concatenate(
        [jnp.dot(qbd, kbuf[slot, i].astype(BF16), preferred_element_type=F32) for i in range(ppb)], axis=1)
    ck = jnp.concatenate([ck_ref[i] for i in range(ppb)], axis=1)
    sc = sc + cq - jnp.concatenate([ck] * T, axis=0)
    m_prev = m_ref[...]
    m_new = jnp.maximum(m_prev, jnp.max(sc, axis=-1, keepdims=True))
    alpha = jnp.exp(m_prev - m_new)
    pf = jnp.exp(sc - m_new)
    l_ref[...] = alpha * l_ref[...] + jnp.sum(pf, axis=-1, keepdims=True)
    p = pf.astype(BF16)
    pv =lax.dot_general(p[:, 0:PAGE], vbuf[slot, 0].astype(BF16), NT_DIMS, preferred_element_type=F32)
    for i in range(1, ppb):
        pv = pv + lax.dot_general(p[:, i * PAGE:(i + 1) * PAGE], vbuf[slot, i].astype(BF16), NT_DIMS,
                                  preferred_element_type=F32)
    acc_ref[...] = alpha * acc_ref[...] + pv
    m_ref[...] = m_new

    @pl.when(s == ns - 1)
    def _():
        cn = cn_ref[...]
        cn_pad = jnp.concatenate([cn, jnp.zeros((PAGE - SUB, LANES), F32)], axis=0)
        cn_row = _exact_rows(_eye_rows(SUB, LANES), cn_pad)
        sn = lax.dot_general(qbd, kn_ref[...].astype(BF16), NT_DIMS, preferred_element_type=F32)
        sn = sn + cq - jnp.concatenate([cn_row] * T, axis=0)
        rt = lax.broadcasted_iota(jnp.int32, (R, PAGE), 0) // F_HEADS
        ct = lax.broadcasted_iota(jnp.int32, (R, PAGE), 1)
        sn = jnp.where((ct <= rt) & (ct < T), sn, NEG)
        update(sn, vn_ref[...].astype(BF16))
        on = acc_ref[...] / l_ref[...]
        for tt in range(T):
            blk = jnp.where(hmask, on[tt * F_HEADS:(tt + 1) * F_HEADS, :], 0.0)
            o_ref[tt:tt + 1, :] = jnp.sum(blk, axis=0, keepdims=True) * _silu(fg_ref[tt:tt + 1, :])
        o_ref[T:SUB, :] = jnp.zeros((SUB - T, F_WIDTH), F32)


def _decode(page_table, qb, kn_pad, vn_pad, lf, tot, ck, z, k_cache, v_cache, *, T, page_off, ppb):
    bd, n_pages = page_table.shape
    ns = n_pages // ppb
    R = T * F_HEADS
    fgc = (Z_FOX + 3 * F_WIDTH) // F_WIDTH
    kern = functools.partial(_decode_kernel, T=T, ppb=ppb, ns=ns, page_off=page_off)
    return pl.pallas_call(
        kern,
        grid_spec=pltpu.PrefetchScalarGridSpec(
            num_scalar_prefetch=1,
            grid=(bd, ns),
            in_specs=[
                pl.BlockSpec((TPAD, F_WIDTH), lambda b, s, pt: (b, 0)),
                pl.BlockSpec((None, PAGE, F_WIDTH), lambda b, s, pt: (b, 0, 0)),
                pl.BlockSpec((None, PAGE, F_WIDTH), lambda b, s, pt: (b, 0, 0)),
                pl.BlockSpec((TPAD, LANES), lambda b, s, pt: (b, 0)),
                pl.BlockSpec((None, 1, LANES), lambda b, s, pt: (b, 0, 0)),
                pl.BlockSpec((None, ppb, F_HEADS, PAGE), lambda b, s, pt: (b, s, 0, 0)),
                pl.BlockSpec((TPAD, F_WIDTH), lambda b, s, pt: (b, fgc)),
                pl.BlockSpec(memory_space=pl.ANY),
                pl.BlockSpec(memory_space=pl.ANY),
            ],
            out_specs=pl.BlockSpec((TPAD, F_WIDTH), lambda b, s, pt: (b, 0)),
            scratch_shapes=[
                pltpu.VMEM((2, ppb, F_WIDTH, PAGE), F32),
                pltpu.VMEM((2, ppb, F_WIDTH, PAGE), F32),
                pltpu.SemaphoreType.DMA((2, 2)),
                pltpu.VMEM((R, F_WIDTH), BF16),
                pltpu.VMEM((R, 1), F32),
                pltpu.VMEM((SUB, LANES), F32),
                pltpu.VMEM((R, 1), F32),
                pltpu.VMEM((R, 1), F32),
                pltpu.VMEM((R, F_WIDTH), F32),
            ],
        ),
        out_shape=jax.ShapeDtypeStruct((bd * TPAD, F_WIDTH), F32),
        compiler_params=_cparams(("arbitrary", "arbitrary")),
        name="decode",
    )(page_table, qb, kn_pad, vn_pad, lf, tot, ck, z, k_cache, v_cache)


def _merge_kernel(x_ref, r_ref, f_ref, m_ref, gate_ref, wr_ref, wf_ref, wm_ref, wo_ref, y_ref):
    br = _bdot(r_ref[...], wr_ref[...])
    bf = _bdot(f_ref[...], wf_ref[...])
    bm = _bdot(m_ref[...], wm_ref[...])
    merged = (jax.nn.sigmoid(gate_ref[:, 0:D_MODEL]) * br
              + jax.nn.sigmoid(gate_ref[:, D_MODEL:2 * D_MODEL]) * bf
              + jax.nn.sigmoid(gate_ref[:, 2 * D_MODEL:3 * D_MODEL]) * bm)
    y_ref[...] = x_ref[...] + _bdot(merged, wo_ref[...])


def _merge(x, r, f, m, z, wr, wf, wm, wo):
    nt = x.shape[0]
    tm = min(nt, 512)
    row = lambda i: (i, 0)
    full = lambda i: (0, 0)
    return pl.pallas_call(
        _merge_kernel,
        grid=(nt // tm,),
        in_specs=[
            pl.BlockSpec((tm, D_MODEL), row),
            pl.BlockSpec((tm, 512), row),
            pl.BlockSpec((tm, 512), row),
            pl.BlockSpec((tm, 512), row),
            pl.BlockSpec((tm, 3 * D_MODEL), lambda i: (i, Z_GATE // (3 * D_MODEL))),
            pl.BlockSpec((512, D_MODEL), full),
            pl.BlockSpec((512, D_MODEL), full),
            pl.BlockSpec((512, D_MODEL), full),
            pl.BlockSpec((D_MODEL, D_MODEL), full),
        ],
        out_specs=pl.BlockSpec((tm, D_MODEL), row),
        out_shape=jax.ShapeDtypeStruct((nt, D_MODEL), F32),
        compiler_params=_cparams(("arbitrary",)),
        name="merge",
    )(x, r, f, m, z, wr, wf, wm, wo)


def _rope_tables(pos):
    inv = ROPE_BASE ** (-jnp.arange(0, R_DK, 2, dtype=F32) / R_DK)
    ang = pos.astype(F32)[:, None] * inv[None, :]
    cos, sin = jnp.cos(ang), jnp.sin(ang)
    cos_h = jnp.concatenate([cos, cos], axis=1)
    sin_h = jnp.concatenate([-sin, sin], axis=1)
    return jnp.tile(cos_h, (1, R_HEADS)), jnp.tile(sin_h, (1, R_HEADS))


def _pad_lanes(v, off=0, width=LANES):
    return jnp.zeros((1, width), F32).at[0, off:off + v.shape[0]].set(v)


def _layer_weights(l, norm_g, w_in, r_norm_g, f_qnorm_g, f_knorm_g, f_bias, m_conv_w, m_conv_b,
                   m_dt_bias, m_A_log, m_D, m_norm_g, w_br_r, w_br_f, w_br_m, w_out):
    w = w_in[l]
    w_main = jnp.concatenate(
        [w[:, 0:1536], w[:, 3592:4616], w[:, 4624:5136], w[:, 5136:8208], w[:, 1536:3072], w[:, 3080:3592]],
        axis=1).astype(BF16)
    zpad = jnp.zeros((D_MODEL, LANES - F_HEADS), F32)
    w_small = jnp.concatenate([w[:, 3072:3080], zpad, w[:, 4616:4624], zpad], axis=1).astype(BF16)
    return dict(
        g=norm_g[l][None, :], w_main=w_main, w_small=w_small,
        gn=r_norm_g[l][None, :],
        gq=jnp.tile(f_qnorm_g[l], F_HEADS)[None, :], gk=jnp.tile(f_knorm_g[l], F_HEADS)[None, :],
        fb=_pad_lanes(f_bias[l]),
        cw=m_conv_w[l], cb=m_conv_b[l][None, :],
        dtb=_pad_lanes(m_dt_bias[l]), alog=_pad_lanes(m_A_log[l]),
        dvec=jnp.repeat(m_D[l], M_HD)[None, :], ng=m_norm_g[l][None, :],
        wr=w_br_r[l].astype(BF16), wf=w_br_f[l].astype(BF16), wm=w_br_m[l].astype(BF16), wo=w_out[l].astype(BF16),
    )


def _consts():
    i = np.arange(F_WIDTH)
    m64 = jnp.asarray((i[:, None] // F_DH == i[None, :] // F_DH).astype(np.float32) / F_DH, dtype=BF16)
    j = np.arange(F_HEADS * LANES)
    expand = jnp.asarray((np.arange(LANES)[:, None] == j[None, :] // LANES).astype(np.float32), dtype=BF16)
    return m64, expand


def kernel(x_prompt, x_sample, cache_fox_k, cache_fox_v, cache_fox_logf, page_table, state_ret, state_conv, state_ssm, norm_g, w_in, r_norm_g, f_qnorm_g, f_knorm_g, f_bias, m_conv_w, m_conv_b, m_dt_bias, m_A_log, m_D, m_norm_g, w_br_r, w_br_f, w_br_m, w_out):
    B, L, D = x_prompt.shape
    Bd, T, _ = x_sample.shape
    depth = w_in.shape[0]
    n_pool = cache_fox_k.shape[1]
    n_pages = page_table.shape[1]
    past = n_pages * PAGE
    C = CHUNK
    n = L // C
    tq = min(L, 512)
    ppb = min(n_pages, 16)

    m64, expand = _consts()
    cos_p, sin_p = _rope_tables(jnp.arange(L))
    cos_s, sin_s = _rope_tables(past + jnp.arange(C))
    kc = jnp.transpose(cache_fox_k, (0, 1, 3, 4, 2)).reshape(depth * n_pool, F_WIDTH, PAGE)
    vc = jnp.transpose(cache_fox_v, (0, 1, 3, 4, 2)).reshape(depth * n_pool, F_WIDTH, PAGE)
    lc = jnp.transpose(cache_fox_logf, (0, 1, 3, 2)).reshape(depth * n_pool, F_HEADS, PAGE)

    hp = x_prompt.reshape(B * L, D)
    hs = jnp.pad(x_sample, ((0, 0), (0, TPAD - T), (0, 0))).reshape(Bd * TPAD, D)
    zeros_ret = jnp.zeros((B, 2, LANES, LANES), F32)
    zeros_conv = jnp.zeros((B, M_CONV - 1, M_CONV_DIM), F32)
    zeros_ssm = jnp.zeros((B, M_HEADS // 2, LANES, LANES), F32)

    def pad_chunk(a):
        w = a.shape[1]
        return jnp.pad(a.reshape(Bd, TPAD, w), ((0, 0), (0, C - TPAD), (0, 0))).reshape(Bd * C, w)

    def unpad_chunk(a):
        w = a.shape[1]
        return a.reshape(Bd, C, w)[:, :TPAD].reshape(Bd * TPAD, w)

    outs_p, outs_s = [], []
    for l in range(depth):
        W = _layer_weights(l, norm_g, w_in, r_norm_g, f_qnorm_g, f_knorm_g, f_bias, m_conv_w, m_conv_b,
                           m_dt_bias, m_A_log, m_D, m_norm_g, w_br_r, w_br_f, w_br_m, w_out)
        z, small = _inproj(hp, W["g"], W["w_main"], W["w_small"])
        r_out, ret_p = _retention(z, Z_RET // 1536, cos_p, sin_p, W["gn"], zeros_ret, B=B, n=n, C=C, T=C)
        qt, kf, kb, vf, vt, lf, ccol, crow = _foxprep_prompt(
            z, small, W["gq"], W["gk"], W["fb"], m64, expand, B=B, nb=L // tq, tm=tq)
        f_out = _foxattn(qt, kb, vt, ccol, crow, z, B=B, L=L, tq=tq)
        m_out, conv_p, ssm_p = _mamba(z, Z_MAM // 1536, small, 1, W["cw"], W["cb"], W["dtb"], W["alog"],
                                      W["dvec"], W["ng"], zeros_conv, zeros_ssm, B=B, n=n, C=C, T=C)
        hp = _merge(hp, r_out, f_out, m_out, z, W["wr"], W["wf"], W["wm"], W["wo"])
        outs_p.append((kf, vf, lf[:, :F_HEADS], ret_p, conv_p, ssm_p))

        zs, smalls = _inproj(hs, W["g"], W["w_main"], W["w_small"])
        r_pad, ret_s = _retention(pad_chunk(zs[:, Z_RET:Z_RET + 1536]), 0, cos_s, sin_s, W["gn"],
                                  state_ret[l].reshape(Bd, 2, LANES, LANES), B=Bd, n=1, C=C, T=T)
        qs, kf, vf, lf = _foxprep_sample(zs, smalls, W["gq"], W["gk"], W["fb"], m64)
        ck, tot = _pagecum(page_table, lc, page_off=l * n_pool)
        f_out = _decode(page_table, qs, pad_chunk(kf).reshape(Bd, C, F_WIDTH), pad_chunk(vf).reshape(Bd, C, F_WIDTH),
                        lf, tot, ck, zs, kc, vc, T=T, page_off=l * n_pool, ppb=ppb)
        m_pad, conv_s, ssm_s = _mamba(pad_chunk(zs[:, Z_MAM:Z_MAM + 1536]), 0, pad_chunk(smalls[:, LANES:]), 0,
                                      W["cw"], W["cb"], W["dtb"], W["alog"], W["dvec"], W["ng"],
                                      state_conv[l], state_ssm[l].reshape(Bd, M_HEADS // 2, LANES, LANES),
                                      B=Bd, n=1, C=C, T=T)
        hs = _merge(hs, unpad_chunk(r_pad), f_out, unpad_chunk(m_pad), zs, W["wr"], W["wf"], W["wm"], W["wo"])
        sel = lambda a: a.reshape(Bd, TPAD, -1)[:, :T]
        outs_s.append((sel(kf), sel(vf), sel(lf)[:, :, :F_HEADS], ret_s, conv_s, ssm_s))

    stk = lambda outs, i: jnp.stack([o[i] for o in outs])
    return (
        hp.reshape(B, L, D),
        hs.reshape(Bd, TPAD, D)[:, :T],
        stk(outs_p, 0).reshape(depth, B, L, F_HEADS, F_DH),
        stk(outs_p, 1).reshape(depth, B, L, F_HEADS, F_DH),
        stk(outs_p, 2).reshape(depth, B, L, F_HEADS),
        stk(outs_p, 3).reshape(depth, B, R_HEADS, R_DK, R_DV),
        stk(outs_p, 4),
        stk(outs_p, 5).reshape(depth, B, M_HEADS, M_HD, M_STATE),
        stk(outs_s, 0).reshape(depth, Bd, T, F_HEADS, F_DH),
        stk(outs_s, 1).reshape(depth, Bd, T, F_HEADS, F_DH),
        stk(outs_s, 2),
        stk(outs_s, 3).reshape(depth, Bd, R_HEADS, R_DK, R_DV),
        stk(outs_s, 4),
        stk(outs_s, 5).reshape(depth, Bd, M_HEADS, M_HD, M_STATE),
    )
```

```python
import functools
import math

import numpy as np
import jax
import jax.numpy as jnp
from jax import lax
from jax.experimental import pallas as pl
from jax.experimental.pallas import tpu as pltpu

F32 = jnp.float32
BF16 = jnp.bfloat16

D_MODEL = 1024
PAGE = 128
R_HEADS, R_DK, R_DV = 4, 64, 128
F_HEADS, F_DH = 8, 64
F_WIDTH = F_HEADS * F_DH
M_HEADS, M_HD, M_GROUPS, M_STATE, M_CONV = 8, 64, 2, 128, 4
M_WIDTH = M_HEADS * M_HD
M_CONV_DIM = M_WIDTH + 2 * M_GROUPS * M_STATE
CHUNK = 128
ROPE_BASE = 10000.0
EPS = 1e-6
NEG = -1e30
LOG2E = 1.4426950408889634

Z_RET, Z_MAM, Z_GATE, Z_FOX = 0, 1536, 3072, 6144
Z_WIDTH = 8192
SMALL_W = 256
LANES = 128
SUB = 8
TPAD = 8

VMEM_LIMIT = 56 * 1024 * 1024

NT_DIMS = (((1,), (1,)), ((), ()))
TN_DIMS = (((0,), (0,)), ((), ()))

LOG_G = [float(np.log1p(-np.exp2(-5.0 - h))) for h in range(R_HEADS)]


def _cparams(sem):
    return pltpu.CompilerParams(dimension_semantics=sem, vmem_limit_bytes=VMEM_LIMIT)


def _silu(x):
    return x * jax.nn.sigmoid(x)


def _softplus(x):
    return jnp.maximum(x, 0.0) + jnp.log1p(jnp.exp(-jnp.abs(x)))


def _log_sigmoid(x):
    return jnp.minimum(x, 0.0) - jnp.log1p(jnp.exp(-jnp.abs(x)))


def _bdot(a, b):
    return jnp.dot(a.astype(BF16), b.astype(BF16), preferred_element_type=F32)


def _bdot_nt(a, b):
    return lax.dot_general(a.astype(BF16), b.astype(BF16), NT_DIMS, preferred_element_type=F32)


def _bdot_tn(a, b):
    return lax.dot_general(a.astype(BF16), b.astype(BF16), TN_DIMS, preferred_element_type=F32)


def _split3(x):
    h1 = x.astype(BF16)
    r1 = x - h1.astype(F32)
    h2 = r1.astype(BF16)
    h3 = (r1 - h2.astype(F32)).astype(BF16)
    return h1, h2, h3


def _exact_dot(x, sel):
    h1, h2, h3 = _split3(x)
    d = lambda a: jnp.dot(a, sel, preferred_element_type=F32)
    return d(h1) + d(h2) + d(h3)


def _exact_rows(sel, x):
    h1, h2, h3 = _split3(x)
    d = lambda a: lax.dot_general(sel, a, NT_DIMS, preferred_element_type=F32)
    return d(h1) + d(h2) + d(h3)


def _cumsum_rows(x):
    n = x.shape[0]
    i = lax.broadcasted_iota(jnp.int32, x.shape, 0)
    s = 1
    while s < n:
        x = x + jnp.where(i >= s, pltpu.roll(x, s, 0), 0.0)
        s *= 2
    return x


def _eye_rows(r, k):
    return (lax.broadcasted_iota(jnp.int32, (r, k), 0) == lax.broadcasted_iota(jnp.int32, (r, k), 1)).astype(BF16)


def _inproj_kernel(x_ref, g_ref, w_ref, ws_ref, z_ref, s_ref, h_ref):
    @pl.when(pl.program_id(1) == 0)
    def _():
        x = x_ref[...]
        r = x * lax.rsqrt(jnp.mean(x * x, axis=-1, keepdims=True) + EPS)
        h = (r * g_ref[...]).astype(BF16)
        h_ref[...] = h
        s_ref[...] = jnp.dot(h, ws_ref[...], preferred_element_type=F32)

    z_ref[...] = jnp.dot(h_ref[...], w_ref[...], preferred_element_type=F32)


def _inproj(x, g, w_main, w_small):
    nt = x.shape[0]
    tm = next((t for t in (2048, 1024, 512, 256) if nt % t == 0), nt)
    tn = 1024
    return pl.pallas_call(
        _inproj_kernel,
        grid=(nt // tm, Z_WIDTH // tn),
        in_specs=[
            pl.BlockSpec((tm, D_MODEL), lambda i, j: (i, 0)),
            pl.BlockSpec((1, D_MODEL), lambda i, j: (0, 0)),
            pl.BlockSpec((D_MODEL, tn), lambda i, j: (0, j)),
            pl.BlockSpec((D_MODEL, SMALL_W), lambda i, j: (0, 0)),
        ],
        out_specs=[
            pl.BlockSpec((tm, tn), lambda i, j: (i, j)),
            pl.BlockSpec((tm, SMALL_W), lambda i, j: (i, 0)),
        ],
        out_shape=[jax.ShapeDtypeStruct((nt, Z_WIDTH), F32), jax.ShapeDtypeStruct((nt, SMALL_W), F32)],
        scratch_shapes=[pltpu.VMEM((tm, D_MODEL), BF16)],
        compiler_params=_cparams(("arbitrary", "arbitrary")),
        name="inproj",
    )(x, g, w_main, w_small)


def _ret_kernel(z_ref, cos_ref, sin_ref, gn_ref, s0_ref, o_ref, sl_ref, st_ref, *, C, T):
    c = pl.program_id(1)

    @pl.when(c == 0)
    def _():
        st_ref[...] = s0_ref[...]

    lane2 = lax.broadcasted_iota(jnp.int32, (C, 2 * LANES), 1)
    first = (lane2 % R_DK) < (R_DK // 2)
    cos = cos_ref[...]
    sin = sin_ref[...]

    def rope(x):
        xr = jnp.where(first, pltpu.roll(x, 2 * LANES - R_DK // 2, 1), pltpu.roll(x, R_DK // 2, 1))
        return x * cos + xr * sin

    q = rope(z_ref[:, 0:256]) * (R_DK ** -0.5)
    k = rope(z_ref[:, 256:512])

    ri = lax.broadcasted_iota(jnp.int32, (C, 1), 0)
    rif = ri.astype(F32)
    valid = ri < T
    di = lax.broadcasted_iota(jnp.int32, (C, C), 0)
    dj = lax.broadcasted_iota(jnp.int32, (C, C), 1)
    dd = jnp.maximum(di - dj, 0).astype(F32)
    causal = di >= dj
    lane = lax.broadcasted_iota(jnp.int32, (C, LANES), 1)
    lo = lane < R_DK
    row = lax.broadcasted_iota(jnp.int32, (LANES, LANES), 0)
    rlo = row < R_DK

    for p in range(R_HEADS // 2):
        qp = q[:, p * LANES:(p + 1) * LANES]
        kp = jnp.where(valid, k[:, p * LANES:(p + 1) * LANES], 0.0)
        sp = st_ref[p]
        ds = []
        for e in range(2):
            h = 2 * p + e
            lg = LOG_G[h]
            qm = jnp.where(lo if e == 0 else jnp.logical_not(lo), qp, 0.0)
            decay = jnp.where(causal, jnp.exp(lg * dd), 0.0)
            wq = jnp.exp(lg * (rif + 1.0))
            wk = jnp.exp(lg * (float(T - 1) - rif))
            vh = z_ref[:, 512 + h * LANES:512 + (h + 1) * LANES]
            s = _bdot_nt(qm, kp) * decay
            o = _bdot(s, vh) + _bdot(qm * wq, sp)
            mu = jnp.mean(o, axis=-1, keepdims=True)
            oc = o - mu
            var = jnp.mean(oc * oc, axis=-1, keepdims=True)
            gate = z_ref[:, 1024 + h * LANES:1024 + (h + 1) * LANES]
            o_ref[:, h * LANES:(h + 1) * LANES] = (
                oc * lax.rsqrt(var + EPS) * gn_ref[:, h * LANES:(h + 1) * LANES] * _silu(gate)).astype(o_ref.dtype)
            ds.append(_bdot_tn(kp * wk, vh))
        gdec = jnp.where(rlo, math.exp(LOG_G[2 * p] * T), math.exp(LOG_G[2 * p + 1] * T))
        st_ref[p] = gdec * sp + jnp.where(rlo, ds[0], ds[1])

    @pl.when(c == pl.num_programs(1) - 1)
    def _():
        sl_ref[...] = st_ref[...]


def _retention(z, zcol, cos, sin, gn, s0, *, B, n, C, T):
    nt = B * n * C
    kern = functools.partial(_ret_kernel, C=C, T=T)
    return pl.pallas_call(
        kern,
        grid=(B, n),
        in_specs=[
            pl.BlockSpec((C, 1536), lambda b, c: (b * n + c, zcol)),
            pl.BlockSpec((C, 256), lambda b, c: (c, 0)),
            pl.BlockSpec((C, 256), lambda b, c: (c, 0)),
            pl.BlockSpec((1, 512), lambda b, c: (0, 0)),
            pl.BlockSpec((None, 2, LANES, LANES), lambda b, c: (b, 0, 0, 0)),
        ],
        out_specs=[
            pl.BlockSpec((C, 512), lambda b, c: (b * n + c, 0)),
            pl.BlockSpec((None, 2, LANES, LANES), lambda b, c: (b, 0, 0, 0)),
        ],
        out_shape=[jax.ShapeDtypeStruct((nt, 512), BF16), jax.ShapeDtypeStruct((B, 2, LANES, LANES), F32)],
        scratch_shapes=[pltpu.VMEM((2, LANES, LANES), F32)],
        compiler_params=_cparams(("arbitrary", "arbitrary")),
        name="retention",
    )(z, cos, sin, gn, s0)


def _mamba_kernel(z_ref, dt_ref, cw_ref, cb_ref, dtb_ref, alog_ref, dvec_ref, ng_ref, cbuf_ref, h0_ref,
                  o_ref, cnew_ref, hl_ref, xp_ref, st_ref, *, C, T):
    c = pl.program_id(1)
    K = M_CONV

    @pl.when(c == 0)
    def _():
        xp_ref[SUB - (K - 1):SUB, :] = cbuf_ref[...]
        st_ref[...] = h0_ref[...]

    xp_ref[SUB:SUB + C, :] = z_ref[:, 0:M_CONV_DIM]
    conv = cb_ref[...]
    for j in range(K):
        conv = conv + xp_ref[SUB - (K - 1) + j:SUB - (K - 1) + j + C, :] * cw_ref[j:j + 1, :]
    tail = xp_ref[SUB + T - (K - 1):SUB + T, :]
    xp_ref[SUB - (K - 1):SUB, :] = tail
    xbc = _silu(conv)

    lane = lax.broadcasted_iota(jnp.int32, (C, LANES), 1)
    ri = lax.broadcasted_iota(jnp.int32, (C, LANES), 0)
    dt = _softplus(dt_ref[...] + dtb_ref[...])
    dt = jnp.where((lane < M_HEADS) & (ri < T), dt, 0.0)
    a = dt * (-jnp.exp(alog_ref[...]))
    a_cs = _cumsum_rows(a)
    eye8 = _eye_rows(SUB, LANES)
    a_row = _exact_rows(eye8, a_cs)
    dt_row = _exact_rows(eye8, dt)
    a_last = a_cs[C - 1:C, :]
    w_end = jnp.exp(a_last - a_cs) * dt
    ea = jnp.exp(a_cs)
    e_last = jnp.exp(a_last)

    di = lax.broadcasted_iota(jnp.int32, (C, C), 0)
    dj = lax.broadcasted_iota(jnp.int32, (C, C), 1)
    causal = di >= dj
    lo = lane < M_HD
    rlo = lax.broadcasted_iota(jnp.int32, (LANES, LANES), 0) < M_HD

    ys = []
    heads_per_group = M_HEADS // M_GROUPS
    for g in range(M_GROUPS):
        bg = xbc[:, M_WIDTH + g * M_STATE:M_WIDTH + (g + 1) * M_STATE]
        cg = xbc[:, M_WIDTH + (M_GROUPS + g) * M_STATE:M_WIDTH + (M_GROUPS + g + 1) * M_STATE]
        cbm = _bdot_nt(cg, bg)
        for pp in range(heads_per_group // 2):
            p = g * (heads_per_group // 2) + pp
            xpair = xbc[:, p * LANES:(p + 1) * LANES]
            yh = []
            for e in range(2):
                h = 2 * p + e
                seg = a_cs[:, h:h + 1] - a_row[h:h + 1, :]
                lm = jnp.exp(jnp.where(causal, seg, NEG))
                m = cbm * lm * dt_row[h:h + 1, :]
                yh.append(_bdot(m, xpair))
            sp = st_ref[p]
            ycross = _bdot_nt(cg, sp) * jnp.where(lo, ea[:, 2 * p:2 * p + 1], ea[:, 2 * p + 1:2 * p + 2])
            y = jnp.where(lo, yh[0], yh[1]) + ycross + dvec_ref[:, p * LANES:(p + 1) * LANES] * xpair
            ys.append(y)
            xs = xpair * jnp.where(lo, w_end[:, 2 * p:2 * p + 1], w_end[:, 2 * p + 1:2 * p + 2])
            dstate = _bdot_tn(xs, bg)
            dec = jnp.where(rlo, e_last[:, 2 * p:2 * p + 1], e_last[:, 2 * p + 1:2 * p + 2])
            st_ref[p] = dec * sp + dstate

    my = jnp.concatenate(ys, axis=1)
    gated = my * _silu(z_ref[:, M_CONV_DIM:M_CONV_DIM + M_WIDTH])
    o_ref[...] = (gated * lax.rsqrt(jnp.mean(gated * gated, axis=-1, keepdims=True) + EPS) * ng_ref[...]).astype(o_ref.dtype)

    @pl.when(c == pl.num_programs(1) - 1)
    def _():
        cnew_ref[...] = xp_ref[SUB - (K - 1):SUB, :]
        hl_ref[...] = st_ref[...]


def _mamba(z, zcol, small, scol, cw, cb, dtb, alog, dvec, ng, cbuf, h0, *, B, n, C, T):
    nt = B * n * C
    kern = functools.partial(_mamba_kernel, C=C, T=T)
    np_ = M_HEADS // 2
    return pl.pallas_call(
        kern,
        grid=(B, n),
        in_specs=[
            pl.BlockSpec((C, 1536), lambda b, c: (b * n + c, zcol)),
            pl.BlockSpec((C, LANES), lambda b, c: (b * n + c, scol)),
            pl.BlockSpec((M_CONV, M_CONV_DIM), lambda b, c: (0, 0)),
            pl.BlockSpec((1, M_CONV_DIM), lambda b, c: (0, 0)),
            pl.BlockSpec((1, LANES), lambda b, c: (0, 0)),
            pl.BlockSpec((1, LANES), lambda b, c: (0, 0)),
            pl.BlockSpec((1, M_WIDTH), lambda b, c: (0, 0)),
            pl.BlockSpec((1, M_WIDTH), lambda b, c: (0, 0)),
            pl.BlockSpec((None, M_CONV - 1, M_CONV_DIM), lambda b, c: (b, 0, 0)),
            pl.BlockSpec((None, np_, LANES, LANES), lambda b, c: (b, 0, 0, 0)),
        ],
        out_specs=[
            pl.BlockSpec((C, M_WIDTH), lambda b, c: (b * n + c, 0)),
            pl.BlockSpec((None, M_CONV - 1, M_CONV_DIM), lambda b, c: (b, 0, 0)),
            pl.BlockSpec((None, np_, LANES, LANES), lambda b, c: (b, 0, 0, 0)),
        ],
        out_shape=[
            jax.ShapeDtypeStruct((nt, M_WIDTH), BF16),
            jax.ShapeDtypeStruct((B, M_CONV - 1, M_CONV_DIM), F32),
            jax.ShapeDtypeStruct((B, np_, LANES, LANES), F32),
        ],
        scratch_shapes=[pltpu.VMEM((SUB + C, M_CONV_DIM), F32), pltpu.VMEM((np_, LANES, LANES), F32)],
        compiler_params=_cparams(("arbitrary", "arbitrary")),
        name="mamba",
    )(z, small, cw, cb, dtb, alog, dvec, ng, cbuf, h0)


def _fox_qkv(z_ref, ff_ref, gq_ref, gk_ref, fb_ref, m64_ref):
    m64 = m64_ref[...]

    def headnorm(x, g):
        x2 = x * x
        hi = x2.astype(BF16)
        lo = (x2 - hi.astype(F32)).astype(BF16)
        ms = jnp.dot(hi, m64, preferred_element_type=F32) + jnp.dot(lo, m64, preferred_element_type=F32)
        return x * lax.rsqrt(ms + EPS) * g

    qs = headnorm(z_ref[:, 0:F_WIDTH], gq_ref[...]) * (F_DH ** -0.5)
    kn = headnorm(z_ref[:, F_WIDTH:2 * F_WIDTH], gk_ref[...])
    v = z_ref[:, 2 * F_WIDTH:3 * F_WIDTH]
    lane = lax.broadcasted_iota(jnp.int32, ff_ref.shape, 1)
    logf = jnp.where(lane < F_HEADS, _log_sigmoid(ff_ref[...] + fb_ref[...]), 0.0)
    return qs, kn, v, logf


def _foxprep_prompt_kernel(z_ref, ff_ref, gq_ref, gk_ref, fb_ref, m64_ref, exp_ref,
                           qt_ref, kb_ref, kft_ref, vft_ref, vt_ref, lft_ref, ccol_ref, crow_ref, carry_ref, *, tm):
    qs, kn, v, logf = _fox_qkv(z_ref, ff_ref, gq_ref, gk_ref, fb_ref, m64_ref)
    kb_ref[...] = kn.astype(BF16)
    eye8 = _eye_rows(SUB, LANES)
    lft_ref[...] = _exact_rows(eye8, logf)
    qs2 = qs * LOG2E
    for p in range(F_HEADS // 2):
        cols = slice(p * LANES, (p + 1) * LANES)
        qt_ref[p] = jnp.transpose(qs2[:, cols]).astype(BF16)
        kft_ref[cols, :] = jnp.transpose(kn[:, cols])
        vtp = jnp.transpose(v[:, cols])
        vft_ref[cols, :] = vtp
        vt_ref[p] = vtp.astype(BF16)

    @pl.when(pl.program_id(1) == 0)
    def _():
        carry_ref[...] = jnp.zeros_like(carry_ref)

    cs = _cumsum_rows(logf) + carry_ref[...]
    carry_ref[...] = cs[tm - 1:tm, :]
    cs2 = cs * LOG2E
    ccol_ref[...] = _exact_dot(cs2, exp_ref[...])
    crow = _exact_rows(eye8, cs2)
    for p in range(F_HEADS // 2):
        crow_ref[p] = crow[2 * p:2 * p + 2, :]


def _foxprep_prompt(z, small, gq, gk, fb, m64, expand, *, B, nb, tm):
    nt = B * nb * tm
    kern = functools.partial(_foxprep_prompt_kernel, tm=tm)
    zc = Z_FOX // 2048
    npair = F_HEADS // 2
    row = lambda b, i: (b * nb + i, 0)
    const = lambda b, i: (0, 0)
    return pl.pallas_call(
        kern,
        grid=(B, nb),
        in_specs=[
            pl.BlockSpec((tm, 2048), lambda b, i: (b * nb + i, zc)),
            pl.BlockSpec((tm, LANES), row),
            pl.BlockSpec((1, F_WIDTH), const),
            pl.BlockSpec((1, F_WIDTH), const),
            pl.BlockSpec((1, LANES), const),
            pl.BlockSpec((F_WIDTH, F_WIDTH), const),
            pl.BlockSpec((LANES, F_HEADS * LANES), const),
        ],
        out_specs=[
            pl.BlockSpec((None, npair, None, LANES, tm), lambda b, i: (b, 0, i, 0, 0)),
            pl.BlockSpec((tm, F_WIDTH), row),
            pl.BlockSpec((None, F_WIDTH, tm), lambda b, i: (b, 0, i)),
            pl.BlockSpec((None, F_WIDTH, tm), lambda b, i: (b, 0, i)),
            pl.BlockSpec((None, npair, None, LANES, tm), lambda b, i: (b, 0, i, 0, 0)),
            pl.BlockSpec((None, F_HEADS, tm), lambda b, i: (b, 0, i)),
            pl.BlockSpec((tm, F_HEADS * LANES), row),
            pl.BlockSpec((None, None, npair, 2, tm), lambda b, i: (b, i, 0, 0, 0)),
        ],
        out_shape=[
            jax.ShapeDtypeStruct((B, npair, nb, LANES, tm), BF16),
            jax.ShapeDtypeStruct((nt, F_WIDTH), BF16),
            jax.ShapeDtypeStruct((B, F_WIDTH, nb * tm), F32),
            jax.ShapeDtypeStruct((B, F_WIDTH, nb * tm), F32),
            jax.ShapeDtypeStruct((B, npair, nb, LANES, tm), BF16),
            jax.ShapeDtypeStruct((B, F_HEADS, nb * tm), F32),
            jax.ShapeDtypeStruct((nt, F_HEADS * LANES), F32),
            jax.ShapeDtypeStruct((B, nb, npair, 2, tm), F32),
        ],
        scratch_shapes=[pltpu.VMEM((1, LANES), F32)],
        compiler_params=_cparams(("arbitrary", "arbitrary")),
        name="foxprep_prompt",
    )(z, small, gq, gk, fb, m64, expand)


def _foxprep_sample_kernel(z_ref, ff_ref, gq_ref, gk_ref, fb_ref, m64_ref, qs_ref, kf_ref, vf_ref, lf_ref):
    qs, kn, v, logf = _fox_qkv(z_ref, ff_ref, gq_ref, gk_ref, fb_ref, m64_ref)
    qs_ref[...] = qs
    kf_ref[...] = kn
    vf_ref[...] = v
    lf_ref[...] = logf


def _foxprep_sample(z, small, gq, gk, fb, m64):
    nt = z.shape[0]
    zc = Z_FOX // 2048
    row = lambda i: (i, 0)
    const = lambda i: (0, 0)
    return pl.pallas_call(
        _foxprep_sample_kernel,
        grid=(1,),
        in_specs=[
            pl.BlockSpec((nt, 2048), lambda i: (i, zc)),
            pl.BlockSpec((nt, LANES), row),
            pl.BlockSpec((1, F_WIDTH), const),
            pl.BlockSpec((1, F_WIDTH), const),
            pl.BlockSpec((1, LANES), const),
            pl.BlockSpec((F_WIDTH, F_WIDTH), const),
        ],
        out_specs=[
            pl.BlockSpec((nt, F_WIDTH), row),
            pl.BlockSpec((nt, F_WIDTH), row),
            pl.BlockSpec((nt, F_WIDTH), row),
            pl.BlockSpec((nt, LANES), row),
        ],
        out_shape=[
            jax.ShapeDtypeStruct((nt, F_WIDTH), F32),
            jax.ShapeDtypeStruct((nt, F_WIDTH), F32),
            jax.ShapeDtypeStruct((nt, F_WIDTH), F32),
            jax.ShapeDtypeStruct((nt, LANES), F32),
        ],
        compiler_params=_cparams(("arbitrary",)),
        name="foxprep_sample",
    )(z, small, gq, gk, fb, m64)


def _foxattn_kernel(qt_ref, k_ref, vt_ref, cq_ref, cke_ref, cko_ref, fg_ref, o_ref, m_ref, l_ref, acc_ref, *, tq):
    qi = pl.program_id(2)
    qt = qt_ref[...]
    rlo = lax.broadcasted_iota(jnp.int32, (LANES, tq), 0) < F_DH
    zero = jnp.zeros_like(qt)
    q2t = jnp.concatenate([jnp.where(rlo, qt, zero), jnp.where(rlo, zero, qt)], axis=1)
    cq2 = jnp.concatenate([cq_ref[0:1, :], cq_ref[1:2, :]], axis=1)
    m_ref[...] = jnp.full_like(m_ref, NEG)
    l_ref[...] = jnp.zeros_like(l_ref)
    acc_ref[...] = jnp.zeros_like(acc_ref)
    kpos = lax.broadcasted_iota(jnp.int32, (tq, 2 * tq), 0)
    qpos = lax.broadcasted_iota(jnp.int32, (tq, 2 * tq), 1) % tq
    causal = kpos <= qpos
    rep = tq // LANES

    def block(j, masked):
        start = pl.multiple_of(j * tq, tq)
        k = k_ref[pl.ds(start, tq), :]
        st = jnp.dot(k, q2t, preferred_element_type=F32)
        cke = cke_ref[pl.ds(start, tq), :]
        cko = cko_ref[pl.ds(start, tq), :]
        u = st - jnp.concatenate([cke] * rep + [cko] * rep, axis=1)
        if masked:
            u = jnp.where(causal, u, NEG)
        m_prev = m_ref[...]
        m_new = jnp.maximum(m_prev, jnp.max(u, axis=0, keepdims=True) + cq2)
        alpha = jnp.exp2(m_prev - m_new)
        p = jnp.exp2(u + (cq2 - m_new))
        l_ref[...] = alpha * l_ref[...] + jnp.sum(p, axis=0, keepdims=True)
        acc_ref[...] = alpha * acc_ref[...] + jnp.dot(vt_ref[j], p.astype(BF16), preferred_element_type=F32)
        m_ref[...] = m_new

    def body(j, carry):
        block(j, False)
        return carry

    lax.fori_loop(0, qi, body, 0)
    block(qi, True)
    on = acc_ref[...] / l_ref[...]
    ot = jnp.concatenate([on[0:F_DH, 0:tq], on[F_DH:2 * F_DH, tq:2 * tq]], axis=0)
    o_ref[...] = (jnp.transpose(ot) * _silu(fg_ref[...])).astype(o_ref.dtype)


def _foxattn(qt, kb, vt, ccol, crow, z, *, B, L, tq):
    nq = L // tq
    npair = F_HEADS // 2
    fgc = (Z_FOX + 3 * F_WIDTH) // LANES
    kern = functools.partial(_foxattn_kernel, tq=tq)
    return pl.pallas_call(
        kern,
        grid=(B, npair, nq),
        in_specs=[
            pl.BlockSpec((None, None, None, LANES, tq), lambda b, p, i: (b, p, i, 0, 0)),
            pl.BlockSpec((L, LANES), lambda b, p, i: (b, p)),
            pl.BlockSpec((None, None, nq, LANES, tq), lambda b, p, i: (b, p, 0, 0, 0)),
            pl.BlockSpec((None, None, None, 2, tq), lambda b, p, i: (b, i, p, 0, 0)),
            pl.BlockSpec((L, LANES), lambda b, p, i: (b, 2 * p)),
            pl.BlockSpec((L, LANES), lambda b, p, i: (b, 2 * p + 1)),
            pl.BlockSpec((tq, LANES), lambda b, p, i: (b * nq + i, fgc + p)),
        ],
        out_specs=pl.BlockSpec((tq, LANES), lambda b, p, i: (b * nq + i, p)),
        out_shape=jax.ShapeDtypeStruct((B * L, F_WIDTH), BF16),
        scratch_shapes=[pltpu.VMEM((1, 2 * tq), F32), pltpu.VMEM((1, 2 * tq), F32), pltpu.VMEM((LANES, 2 * tq), F32)],
        compiler_params=_cparams(("arbitrary", "arbitrary", "arbitrary")),
        name="foxattn",
    )(qt, kb, vt, crow, ccol, ccol, z)


def _pagecum_kernel(pt_ref, lf_hbm, ck_ref, tot_ref, g_ref, sem, *, n_pages, page_off):
    b = pl.program_id(0)

    def copy(bb, pg, slot):
        return pltpu.make_async_copy(lf_hbm.at[pt_ref[bb, pg] + page_off], g_ref.at[slot, pg], sem.at[slot])

    def start_all(bb, slot):
        def body(pg, c):
            copy(bb, pg, slot).start()
            return c
        lax.fori_loop(0, n_pages, body, 0)

    @pl.when(b == 0)
    def _():
        start_all(b, 0)

    @pl.when(b + 1 < pl.num_programs(0))
    def _():
        start_all(b + 1, (b + 1) % 2)

    slot = b % 2

    def wait(pg, c):
        copy(b, pg, slot).wait()
        return c

    lax.fori_loop(0, n_pages, wait, 0)

    rows = n_pages * F_HEADS
    w = g_ref[slot].reshape(rows, PAGE)
    lane = lax.broadcasted_iota(jnp.int32, (rows, PAGE), 1)
    row = lax.broadcasted_iota(jnp.int32, (rows, PAGE), 0)
    s = 1
    while s < PAGE:
        w = w + jnp.where(lane >= s, pltpu.roll(w, s, 1), 0.0)
        s *= 2
    tot = jnp.broadcast_to(w[:, PAGE - 1:PAGE], (rows, PAGE))
    inc = tot
    s = F_HEADS
    while s < rows:
        inc = inc + jnp.where(row >= s, pltpu.roll(inc, s, 0), 0.0)
        s *= 2
    ck_ref[...] = (w + (inc - tot)).reshape(n_pages, F_HEADS, PAGE)
    last = inc[rows - F_HEADS:rows, :]
    eye = (lax.broadcasted_iota(jnp.int32, (F_HEADS, LANES), 0) == lax.broadcasted_iota(jnp.int32, (F_HEADS, LANES), 1))
    tot_ref[...] = jnp.sum(jnp.where(eye, last, 0.0), axis=0, keepdims=True)


def _pagecum(page_table, lf_cache, *, page_off):
    bd, n_pages = page_table.shape
    kern = functools.partial(_pagecum_kernel, n_pages=n_pages, page_off=page_off)
    return pl.pallas_call(
        kern,
        grid_spec=pltpu.PrefetchScalarGridSpec(
            num_scalar_prefetch=1,
            grid=(bd,),
            in_specs=[pl.BlockSpec(memory_space=pl.ANY)],
            out_specs=[
                pl.BlockSpec((None, n_pages, F_HEADS, PAGE), lambda b, pt: (b, 0, 0, 0)),
                pl.BlockSpec((None, 1, LANES), lambda b, pt: (b, 0, 0)),
            ],
            scratch_shapes=[pltpu.VMEM((2, n_pages, F_HEADS, PAGE), F32), pltpu.SemaphoreType.DMA((2,))],
        ),
        out_shape=[
            jax.ShapeDtypeStruct((bd, n_pages, F_HEADS, PAGE), F32),
            jax.ShapeDtypeStruct((bd, 1, LANES), F32),
        ],
        compiler_params=_cparams(("arbitrary",)),
        name="pagecum",
    )(page_table, lf_cache)


def _decode_kernel(pt_ref, q_ref, kn_ref, vn_ref, lfn_ref, tot_ref, ck_ref, fg_ref, k_hbm, v_hbm,
                   o_ref, kbuf, vbuf, sem, qbd_ref, cq_ref, cn_ref, m_ref, l_ref, acc_ref,
                   *, T, ppb, ns, page_off):
    b = pl.program_id(0)
    s = pl.program_id(1)
    t = b * ns + s
    nt = pl.num_programs(0) * ns
    R = T * F_HEADS

    def copies(bb, ss, slot):
        out = []
        for i in range(ppb):
            page = pt_ref[bb, ss * ppb + i] + page_off
            out.append(pltpu.make_async_copy(k_hbm.at[page], kbuf.at[slot, i], sem.at[0, slot]))
            out.append(pltpu.make_async_copy(v_hbm.at[page], vbuf.at[slot, i], sem.at[1, slot]))
        return out

    @pl.when(t == 0)
    def _():
        for cp in copies(b, s, 0):
            cp.start()

    @pl.when(t + 1 < nt)
    def _():
        t1 = t + 1
        for cp in copies(t1 // ns, t1 % ns, t1 % 2):
            cp.start()

    rl = lax.broadcasted_iota(jnp.int32, (SUB, F_WIDTH), 1) // F_DH
    rh = lax.broadcasted_iota(jnp.int32, (SUB, F_WIDTH), 0)
    hmask = rl == rh

    @pl.when(s == 0)
    def _():
        q = q_ref[...].astype(F32)
        qbd_ref[...] = jnp.concatenate(
            [jnp.where(hmask, jnp.broadcast_to(q[tt:tt + 1, :], (SUB, F_WIDTH)), 0.0) for tt in range(T)],
            axis=0).astype(BF16)
        cn = tot_ref[...] + _cumsum_rows(lfn_ref[...])
        cn_ref[...] = cn
        eye = (lax.broadcasted_iota(jnp.int32, (SUB, LANES), 0) == lax.broadcasted_iota(jnp.int32, (SUB, LANES), 1))
        cq_ref[...] = jnp.concatenate(
            [jnp.sum(jnp.where(eye, jnp.broadcast_to(cn[tt:tt + 1, :], (SUB, LANES)), 0.0), axis=-1, keepdims=True)
             for tt in range(T)], axis=0)
        m_ref[...] = jnp.full_like(m_ref, NEG)
        l_ref[...] = jnp.zeros_like(l_ref)
        acc_ref[...] = jnp.zeros_like(acc_ref)

    slot = t % 2
    for cp in copies(b, s, slot):
        cp.wait()

    def update(sc, v):
        m_prev = m_ref[...]
        m_new = jnp.maximum(m_prev, jnp.max(sc, axis=-1, keepdims=True))
        alpha = jnp.exp(m_prev - m_new)
        p = jnp.exp(sc - m_new)
        l_ref[...] = alpha * l_ref[...] + jnp.sum(p, axis=-1, keepdims=True)
        acc_ref[...] = alpha * acc_ref[...] + jnp.dot(p.astype(BF16), v, preferred_element_type=F32)
        m_ref[...] = m_new

    qbd = qbd_ref[...]
    cq = cq_ref[...]
    sc = jnp.concatenate(
        [jnp.dot(qbd, kbuf[slot, i].astype(BF16), preferred_element_type=F32) for i in range(ppb)], axis=1)
    ck = jnp.concatenate([ck_ref[i] for i in range(ppb)], axis=1)
    sc = sc + cq - jnp.concatenate([ck] * T, axis=0)
    m_prev = m_ref[...]
    m_new = jnp.maximum(m_prev, jnp.max(sc, axis=-1, keepdims=True))
    alpha = jnp.exp(m_prev - m_new)
    pf = jnp.exp(sc - m_new)
    l_ref[...] = alpha * l_ref[...] + jnp.sum(pf, axis=-1, keepdims=True)
    p = pf.astype(BF16)
    pv =lax.dot_general(p[:, 0:PAGE], vbuf[slot, 0].astype(BF16), NT_DIMS, preferred_element_type=F32)
    for i in range(1, ppb):
        pv = pv + lax.dot_general(p[:, i * PAGE:(i + 1) * PAGE], vbuf[slot, i].astype(BF16), NT_DIMS,
                                  preferred_element_type=F32)
    acc_ref[...] = alpha * acc_ref[...] + pv
    m_ref[...] = m_new

    @pl.when(s == ns - 1)
    def _():
        cn = cn_ref[...]
        cn_pad = jnp.concatenate([cn, jnp.zeros((PAGE - SUB, LANES), F32)], axis=0)
        cn_row = _exact_rows(_eye_rows(SUB, LANES), cn_pad)
        sn = lax.dot_general(qbd, kn_ref[...].astype(BF16), NT_DIMS, preferred_element_type=F32)
        sn = sn + cq - jnp.concatenate([cn_row] * T, axis=0)
        rt = lax.broadcasted_iota(jnp.int32, (R, PAGE), 0) // F_HEADS
        ct = lax.broadcasted_iota(jnp.int32, (R, PAGE), 1)
        sn = jnp.where((ct <= rt) & (ct < T), sn, NEG)
        update(sn, vn_ref[...].astype(BF16))
        on = acc_ref[...] / l_ref[...]
        for tt in range(T):
            blk = jnp.where(hmask, on[tt * F_HEADS:(tt + 1) * F_HEADS, :], 0.0)
            o_ref[tt:tt + 1, :] = jnp.sum(blk, axis=0, keepdims=True) * _silu(fg_ref[tt:tt + 1, :])
        o_ref[T:SUB, :] = jnp.zeros((SUB - T, F_WIDTH), F32)


def _decode(page_table, qb, kn_pad, vn_pad, lf, tot, ck, z, k_cache, v_cache, *, T, page_off, ppb):
    bd, n_pages = page_table.shape
    ns = n_pages // ppb
    R = T * F_HEADS
    fgc = (Z_FOX + 3 * F_WIDTH) // F_WIDTH
    kern = functools.partial(_decode_kernel, T=T, ppb=ppb, ns=ns, page_off=page_off)
    return pl.pallas_call(
        kern,
        grid_spec=pltpu.PrefetchScalarGridSpec(
            num_scalar_prefetch=1,
            grid=(bd, ns),
            in_specs=[
                pl.BlockSpec((TPAD, F_WIDTH), lambda b, s, pt: (b, 0)),
                pl.BlockSpec((None, PAGE, F_WIDTH), lambda b, s, pt: (b, 0, 0)),
                pl.BlockSpec((None, PAGE, F_WIDTH), lambda b, s, pt: (b, 0, 0)),
                pl.BlockSpec((TPAD, LANES), lambda b, s, pt: (b, 0)),
                pl.BlockSpec((None, 1, LANES), lambda b, s, pt: (b, 0, 0)),
                pl.BlockSpec((None, ppb, F_HEADS, PAGE), lambda b, s, pt: (b, s, 0, 0)),
                pl.BlockSpec((TPAD, F_WIDTH), lambda b, s, pt: (b, fgc)),
                pl.BlockSpec(memory_space=pl.ANY),
                pl.BlockSpec(memory_space=pl.ANY),
            ],
            out_specs=pl.BlockSpec((TPAD, F_WIDTH), lambda b, s, pt: (b, 0)),
            scratch_shapes=[
                pltpu.VMEM((2, ppb, F_WIDTH, PAGE), F32),
                pltpu.VMEM((2, ppb, F_WIDTH, PAGE), F32),
                pltpu.SemaphoreType.DMA((2, 2)),
                pltpu.VMEM((R, F_WIDTH), BF16),
                pltpu.VMEM((R, 1), F32),
                pltpu.VMEM((SUB, LANES), F32),
                pltpu.VMEM((R, 1), F32),
                pltpu.VMEM((R, 1), F32),
                pltpu.VMEM((R, F_WIDTH), F32),
            ],
        ),
        out_shape=jax.ShapeDtypeStruct((bd * TPAD, F_WIDTH), F32),
        compiler_params=_cparams(("arbitrary", "arbitrary")),
        name="decode",
    )(page_table, qb, kn_pad, vn_pad, lf, tot, ck, z, k_cache, v_cache)


def _merge_kernel(x_ref, r_ref, f_ref, m_ref, gate_ref, wr_ref, wf_ref, wm_ref, wo_ref, y_ref):
    br = _bdot(r_ref[...], wr_ref[...])
    bf = _bdot(f_ref[...], wf_ref[...])
    bm = _bdot(m_ref[...], wm_ref[...])
    merged = (jax.nn.sigmoid(gate_ref[:, 0:D_MODEL]) * br
              + jax.nn.sigmoid(gate_ref[:, D_MODEL:2 * D_MODEL]) * bf
              + jax.nn.sigmoid(gate_ref[:, 2 * D_MODEL:3 * D_MODEL]) * bm)
    y_ref[...] = x_ref[...] + _bdot(merged, wo_ref[...])


def _merge(x, r, f, m, z, wr, wf, wm, wo):
    nt = x.shape[0]
    tm = min(nt, 512)
    row = lambda i: (i, 0)
    full = lambda i: (0, 0)
    return pl.pallas_call(
        _merge_kernel,
        grid=(nt // tm,),
        in_specs=[
            pl.BlockSpec((tm, D_MODEL), row),
            pl.BlockSpec((tm, 512), row),
            pl.BlockSpec((tm, 512), row),
            pl.BlockSpec((tm, 512), row),
            pl.BlockSpec((tm, 3 * D_MODEL), lambda i: (i, Z_GATE // (3 * D_MODEL))),
            pl.BlockSpec((512, D_MODEL), full),
            pl.BlockSpec((512, D_MODEL), full),
            pl.BlockSpec((512, D_MODEL), full),
            pl.BlockSpec((D_MODEL, D_MODEL), full),
        ],
        out_specs=pl.BlockSpec((tm, D_MODEL), row),
        out_shape=jax.ShapeDtypeStruct((nt, D_MODEL), F32),
        compiler_params=_cparams(("arbitrary",)),
        name="merge",
    )(x, r, f, m, z, wr, wf, wm, wo)


def _rope_tables(pos):
    inv = ROPE_BASE ** (-jnp.arange(0, R_DK, 2, dtype=F32) / R_DK)
    ang = pos.astype(F32)[:, None] * inv[None, :]
    cos, sin = jnp.cos(ang), jnp.sin(ang)
    cos_h = jnp.concatenate([cos, cos], axis=1)
    sin_h = jnp.concatenate([-sin, sin], axis=1)
    return jnp.tile(cos_h, (1, R_HEADS)), jnp.tile(sin_h, (1, R_HEADS))


def _pad_lanes(v, off=0, width=LANES):
    return jnp.zeros((1, width), F32).at[0, off:off + v.shape[0]].set(v)


def _layer_weights(l, norm_g, w_in, r_norm_g, f_qnorm_g, f_knorm_g, f_bias, m_conv_w, m_conv_b,
                   m_dt_bias, m_A_log, m_D, m_norm_g, w_br_r, w_br_f, w_br_m, w_out):
    w = w_in[l]
    w_main = jnp.concatenate(
        [w[:, 0:1536], w[:, 3592:4616], w[:, 4624:5136], w[:, 5136:8208], w[:, 1536:3072], w[:, 3080:3592]],
        axis=1).astype(BF16)
    zpad = jnp.zeros((D_MODEL, LANES - F_HEADS), F32)
    w_small = jnp.concatenate([w[:, 3072:3080], zpad, w[:, 4616:4624], zpad], axis=1).astype(BF16)
    return dict(
        g=norm_g[l][None, :], w_main=w_main, w_small=w_small,
        gn=r_norm_g[l][None, :],
        gq=jnp.tile(f_qnorm_g[l], F_HEADS)[None, :], gk=jnp.tile(f_knorm_g[l], F_HEADS)[None, :],
        fb=_pad_lanes(f_bias[l]),
        cw=m_conv_w[l], cb=m_conv_b[l][None, :],
        dtb=_pad_lanes(m_dt_bias[l]), alog=_pad_lanes(m_A_log[l]),
        dvec=jnp.repeat(m_D[l], M_HD)[None, :], ng=m_norm_g[l][None, :],
        wr=w_br_r[l].astype(BF16), wf=w_br_f[l].astype(BF16), wm=w_br_m[l].astype(BF16), wo=w_out[l].astype(BF16),
    )


def _consts():
    i = np.arange(F_WIDTH)
    m64 = jnp.asarray((i[:, None] // F_DH == i[None, :] // F_DH).astype(np.float32) / F_DH, dtype=BF16)
    j = np.arange(F_HEADS * LANES)
    expand = jnp.asarray((np.arange(LANES)[:, None] == j[None, :] // LANES).astype(np.float32), dtype=BF16)
    return m64, expand


def kernel(x_prompt, x_sample, cache_fox_k, cache_fox_v, cache_fox_logf, page_table, state_ret, state_conv, state_ssm, norm_g, w_in, r_norm_g, f_qnorm_g, f_knorm_g, f_bias, m_conv_w, m_conv_b, m_dt_bias, m_A_log, m_D, m_norm_g, w_br_r, w_br_f, w_br_m, w_out):
    B, L, D = x_prompt.shape
    Bd, T, _ = x_sample.shape
    depth = w_in.shape[0]
    n_pool = cache_fox_k.shape[1]
    n_pages = page_table.shape[1]
    past = n_pages * PAGE
    C = CHUNK
    n = L // C
    tq = min(L, 512)
    ppb = min(n_pages, 16)

    m64, expand = _consts()
    cos_p, sin_p = _rope_tables(jnp.arange(L))
    cos_s, sin_s = _rope_tables(past + jnp.arange(C))
    kc = jnp.transpose(cache_fox_k, (0, 1, 3, 4, 2)).reshape(depth * n_pool, F_WIDTH, PAGE)
    vc = jnp.transpose(cache_fox_v, (0, 1, 3, 4, 2)).reshape(depth * n_pool, F_WIDTH, PAGE)
    lc = jnp.transpose(cache_fox_logf, (0, 1, 3, 2)).reshape(depth * n_pool, F_HEADS, PAGE)

    hp = x_prompt.reshape(B * L, D)
    hs = jnp.pad(x_sample, ((0, 0), (0, TPAD - T), (0, 0))).reshape(Bd * TPAD, D)
    zeros_ret = jnp.zeros((B, 2, LANES, LANES), F32)
    zeros_conv = jnp.zeros((B, M_CONV - 1, M_CONV_DIM), F32)
    zeros_ssm = jnp.zeros((B, M_HEADS // 2, LANES, LANES), F32)

    def pad_chunk(a):
        w = a.shape[1]
        return jnp.pad(a.reshape(Bd, TPAD, w), ((0, 0), (0, C - TPAD), (0, 0))).reshape(Bd * C, w)

    def unpad_chunk(a):
        w = a.shape[1]
        return a.reshape(Bd, C, w)[:, :TPAD].reshape(Bd * TPAD, w)

    outs_p, outs_s = [], []
    for l in range(depth):
        W = _layer_weights(l, norm_g, w_in, r_norm_g, f_qnorm_g, f_knorm_g, f_bias, m_conv_w, m_conv_b,
                           m_dt_bias, m_A_log, m_D, m_norm_g, w_br_r, w_br_f, w_br_m, w_out)
        z, small = _inproj(hp, W["g"], W["w_main"], W["w_small"])
        r_out, ret_p = _retention(z, Z_RET // 1536, cos_p, sin_p, W["gn"], zeros_ret, B=B, n=n, C=C, T=C)
        qt, kb, kft, vft, vt, lft, ccol, crow = _foxprep_prompt(
            z, small, W["gq"], W["gk"], W["fb"], m64, expand, B=B, nb=L // tq, tm=tq)
        f_out = _foxattn(qt, kb, vt, ccol, crow, z, B=B, L=L, tq=tq)
        m_out, conv_p, ssm_p = _mamba(z, Z_MAM // 1536, small, 1, W["cw"], W["cb"], W["dtb"], W["alog"],
                                      W["dvec"], W["ng"], zeros_conv, zeros_ssm, B=B, n=n, C=C, T=C)
        hp = _merge(hp, r_out, f_out, m_out, z, W["wr"], W["wf"], W["wm"], W["wo"])
        outs_p.append((kft, vft, lft, ret_p, conv_p, ssm_p))

        zs, smalls = _inproj(hs, W["g"], W["w_main"], W["w_small"])
        r_pad, ret_s = _retention(pad_chunk(zs[:, Z_RET:Z_RET + 1536]), 0, cos_s, sin_s, W["gn"],
                                  state_ret[l].reshape(Bd, 2, LANES, LANES), B=Bd, n=1, C=C, T=T)
        qs, kf, vf, lf = _foxprep_sample(zs, smalls, W["gq"], W["gk"], W["fb"], m64)
        ck, tot = _pagecum(page_table, lc, page_off=l * n_pool)
        f_out = _decode(page_table, qs, pad_chunk(kf).reshape(Bd, C, F_WIDTH), pad_chunk(vf).reshape(Bd, C, F_WIDTH),
                        lf, tot, ck, zs, kc, vc, T=T, page_off=l * n_pool, ppb=ppb)
        m_pad, conv_s, ssm_s = _mamba(pad_chunk(zs[:, Z_MAM:Z_MAM + 1536]), 0, pad_chunk(smalls[:, LANES:]), 0,
                                      W["cw"], W["cb"], W["dtb"], W["alog"], W["dvec"], W["ng"],
                                      state_conv[l], state_ssm[l].reshape(Bd, M_HEADS // 2, LANES, LANES),
                                      B=Bd, n=1, C=C, T=T)
        hs = _merge(hs, unpad_chunk(r_pad), f_out, unpad_chunk(m_pad), zs, W["wr"], W["wf"], W["wm"], W["wo"])
        sel = lambda a: a.reshape(Bd, TPAD, -1)[:, :T]
        outs_s.append((sel(kf), sel(vf), sel(lf)[:, :, :F_HEADS], ret_s, conv_s, ssm_s))

    stk = lambda outs, i: jnp.stack([o[i] for o in outs])
    return (
        hp.reshape(B, L, D),
        hs.reshape(Bd, TPAD, D)[:, :T],
        jnp.transpose(stk(outs_p, 0).reshape(depth, B, F_HEADS, F_DH, L), (0, 1, 4, 2, 3)),
        jnp.transpose(stk(outs_p, 1).reshape(depth, B, F_HEADS, F_DH, L), (0, 1, 4, 2, 3)),
        jnp.transpose(stk(outs_p, 2), (0, 1, 3, 2)),
        stk(outs_p, 3).reshape(depth, B, R_HEADS, R_DK, R_DV),
        stk(outs_p, 4),
        stk(outs_p, 5).reshape(depth, B, M_HEADS, M_HD, M_STATE),
        stk(outs_s, 0).reshape(depth, Bd, T, F_HEADS, F_DH),
        stk(outs_s, 1).reshape(depth, Bd, T, F_HEADS, F_DH),
        stk(outs_s, 2),
        stk(outs_s, 3).reshape(depth, Bd, R_HEADS, R_DK, R_DV),
        stk(outs_s, 4),
        stk(outs_s, 5).reshape(depth, Bd, M_HEADS, M_HD, M_STATE),
    )
```

```python
import functools
import math

import numpy as np
import jax
import jax.numpy as jnp
from jax import lax
from jax.experimental import pallas as pl
from jax.experimental.pallas import tpu as pltpu

F32 = jnp.float32
BF16 = jnp.bfloat16

D_MODEL = 1024
PAGE = 128
R_HEADS, R_DK, R_DV = 4, 64, 128
F_HEADS, F_DH = 8, 64
F_WIDTH = F_HEADS * F_DH
M_HEADS, M_HD, M_GROUPS, M_STATE, M_CONV = 8, 64, 2, 128, 4
M_WIDTH = M_HEADS * M_HD
M_CONV_DIM = M_WIDTH + 2 * M_GROUPS * M_STATE
CHUNK = 128
ROPE_BASE = 10000.0
EPS = 1e-6
NEG = -1e30
LOG2E = 1.4426950408889634

Z_RET, Z_MAM, Z_GATE, Z_FOX = 0, 1536, 3072, 6144
Z_WIDTH = 8192
SMALL_W = 256
LANES = 128
SUB = 8
TPAD = 8
STRIP = 256

VMEM_LIMIT = 56 * 1024 * 1024

NT_DIMS = (((1,), (1,)), ((), ()))
TN_DIMS = (((0,), (0,)), ((), ()))

LOG_G = [float(np.log1p(-np.exp2(-5.0 - h))) for h in range(R_HEADS)]


def _cparams(sem):
    return pltpu.CompilerParams(dimension_semantics=sem, vmem_limit_bytes=VMEM_LIMIT)


def _silu(x):
    return x * jax.nn.sigmoid(x)


def _softplus(x):
    return jnp.maximum(x, 0.0) + jnp.log1p(jnp.exp(-jnp.abs(x)))


def _log_sigmoid(x):
    return jnp.minimum(x, 0.0) - jnp.log1p(jnp.exp(-jnp.abs(x)))


def _bdot(a, b):
    return jnp.dot(a.astype(BF16), b.astype(BF16), preferred_element_type=F32)


def _bdot_nt(a, b):
    return lax.dot_general(a.astype(BF16), b.astype(BF16), NT_DIMS, preferred_element_type=F32)


def _bdot_tn(a, b):
    return lax.dot_general(a.astype(BF16), b.astype(BF16), TN_DIMS, preferred_element_type=F32)


def _split3(x):
    h1 = x.astype(BF16)
    r1 = x - h1.astype(F32)
    h2 = r1.astype(BF16)
    h3 = (r1 - h2.astype(F32)).astype(BF16)
    return h1, h2, h3


def _exact_dot(x, sel):
    h1, h2, h3 = _split3(x)
    d = lambda a: jnp.dot(a, sel, preferred_element_type=F32)
    return d(h1) + d(h2) + d(h3)


def _exact_rows(sel, x):
    h1, h2, h3 = _split3(x)
    d = lambda a: lax.dot_general(sel, a, NT_DIMS, preferred_element_type=F32)
    return d(h1) + d(h2) + d(h3)


def _cumsum_rows(x):
    n = x.shape[0]
    i = lax.broadcasted_iota(jnp.int32, x.shape, 0)
    s = 1
    while s < n:
        x = x + jnp.where(i >= s, pltpu.roll(x, s, 0), 0.0)
        s *= 2
    return x


def _eye_rows(r, k):
    return (lax.broadcasted_iota(jnp.int32, (r, k), 0) == lax.broadcasted_iota(jnp.int32, (r, k), 1)).astype(BF16)


def _inproj_kernel(x_ref, g_ref, w_ref, ws_ref, z_ref, s_ref, h_ref):
    @pl.when(pl.program_id(1) == 0)
    def _():
        x = x_ref[...]
        r = x * lax.rsqrt(jnp.mean(x * x, axis=-1, keepdims=True) + EPS)
        h = (r * g_ref[...]).astype(BF16)
        h_ref[...] = h
        s_ref[...] = jnp.dot(h, ws_ref[...], preferred_element_type=F32)

    z_ref[...] = jnp.dot(h_ref[...], w_ref[...], preferred_element_type=F32)


def _inproj(x, g, w_main, w_small):
    nt = x.shape[0]
    tm = next((t for t in (2048, 1024, 512, 256) if nt % t == 0), nt)
    tn = 1024
    return pl.pallas_call(
        _inproj_kernel,
        grid=(nt // tm, Z_WIDTH // tn),
        in_specs=[
            pl.BlockSpec((tm, D_MODEL), lambda i, j: (i, 0)),
            pl.BlockSpec((1, D_MODEL), lambda i, j: (0, 0)),
            pl.BlockSpec((D_MODEL, tn), lambda i, j: (0, j)),
            pl.BlockSpec((D_MODEL, SMALL_W), lambda i, j: (0, 0)),
        ],
        out_specs=[
            pl.BlockSpec((tm, tn), lambda i, j: (i, j)),
            pl.BlockSpec((tm, SMALL_W), lambda i, j: (i, 0)),
        ],
        out_shape=[jax.ShapeDtypeStruct((nt, Z_WIDTH), F32), jax.ShapeDtypeStruct((nt, SMALL_W), F32)],
        scratch_shapes=[pltpu.VMEM((tm, D_MODEL), BF16)],
        compiler_params=_cparams(("arbitrary", "arbitrary")),
        name="inproj",
    )(x, g, w_main, w_small)


def _ret_kernel(z_ref, cos_ref, sin_ref, gn_ref, s0_ref, o_ref, sl_ref, st_ref, *, C, T):
    c = pl.program_id(1)

    @pl.when(c == 0)
    def _():
        st_ref[...] = s0_ref[...]

    lane2 = lax.broadcasted_iota(jnp.int32, (C, 2 * LANES), 1)
    first = (lane2 % R_DK) < (R_DK // 2)
    cos = cos_ref[...]
    sin = sin_ref[...]

    def rope(x):
        xr = jnp.where(first, pltpu.roll(x, 2 * LANES - R_DK // 2, 1), pltpu.roll(x, R_DK // 2, 1))
        return x * cos + xr * sin

    q = rope(z_ref[:, 0:256]) * (R_DK ** -0.5)
    k = rope(z_ref[:, 256:512])

    ri = lax.broadcasted_iota(jnp.int32, (C, 1), 0)
    rif = ri.astype(F32)
    valid = ri < T
    di = lax.broadcasted_iota(jnp.int32, (C, C), 0)
    dj = lax.broadcasted_iota(jnp.int32, (C, C), 1)
    dd = jnp.maximum(di - dj, 0).astype(F32)
    causal = di >= dj
    lane = lax.broadcasted_iota(jnp.int32, (C, LANES), 1)
    lo = lane < R_DK
    row = lax.broadcasted_iota(jnp.int32, (LANES, LANES), 0)
    rlo = row < R_DK

    for p in range(R_HEADS // 2):
        qp = q[:, p * LANES:(p + 1) * LANES]
        kp = jnp.where(valid, k[:, p * LANES:(p + 1) * LANES], 0.0)
        sp = st_ref[p]
        ds = []
        for e in range(2):
            h = 2 * p + e
            lg = LOG_G[h]
            qm = jnp.where(lo if e == 0 else jnp.logical_not(lo), qp, 0.0)
            decay = jnp.where(causal, jnp.exp(lg * dd), 0.0)
            wq = jnp.exp(lg * (rif + 1.0))
            wk = jnp.exp(lg * (float(T - 1) - rif))
            vh = z_ref[:, 512 + h * LANES:512 + (h + 1) * LANES]
            s = _bdot_nt(qm, kp) * decay
            o = _bdot(s, vh) + _bdot(qm * wq, sp)
            mu = jnp.mean(o, axis=-1, keepdims=True)
            oc = o - mu
            var = jnp.mean(oc * oc, axis=-1, keepdims=True)
            gate = z_ref[:, 1024 + h * LANES:1024 + (h + 1) * LANES]
            o_ref[:, h * LANES:(h + 1) * LANES] = (
                oc * lax.rsqrt(var + EPS) * gn_ref[:, h * LANES:(h + 1) * LANES] * _silu(gate)).astype(o_ref.dtype)
            ds.append(_bdot_tn(kp * wk, vh))
        gdec = jnp.where(rlo, math.exp(LOG_G[2 * p] * T), math.exp(LOG_G[2 * p + 1] * T))
        st_ref[p] = gdec * sp + jnp.where(rlo, ds[0], ds[1])

    @pl.when(c == pl.num_programs(1) - 1)
    def _():
        sl_ref[...] = st_ref[...]


def _retention(z, zcol, cos, sin, gn, s0, *, B, n, C, T):
    nt = B * n * C
    kern = functools.partial(_ret_kernel, C=C, T=T)
    return pl.pallas_call(
        kern,
        grid=(B, n),
        in_specs=[
            pl.BlockSpec((C, 1536), lambda b, c: (b * n + c, zcol)),
            pl.BlockSpec((C, 256), lambda b, c: (c, 0)),
            pl.BlockSpec((C, 256), lambda b, c: (c, 0)),
            pl.BlockSpec((1, 512), lambda b, c: (0, 0)),
            pl.BlockSpec((None, 2, LANES, LANES), lambda b, c: (b, 0, 0, 0)),
        ],
        out_specs=[
            pl.BlockSpec((C, 512), lambda b, c: (b * n + c, 0)),
            pl.BlockSpec((None, 2, LANES, LANES), lambda b, c: (b, 0, 0, 0)),
        ],
        out_shape=[jax.ShapeDtypeStruct((nt, 512), BF16), jax.ShapeDtypeStruct((B, 2, LANES, LANES), F32)],
        scratch_shapes=[pltpu.VMEM((2, LANES, LANES), F32)],
        compiler_params=_cparams(("arbitrary", "arbitrary")),
        name="retention",
    )(z, cos, sin, gn, s0)


def _mamba_kernel(z_ref, dt_ref, cw_ref, cb_ref, dtb_ref, alog_ref, dvec_ref, ng_ref, cbuf_ref, h0_ref,
                  o_ref, cnew_ref, hl_ref, xp_ref, st_ref, *, C, T):
    c = pl.program_id(1)
    K = M_CONV

    @pl.when(c == 0)
    def _():
        xp_ref[SUB - (K - 1):SUB, :] = cbuf_ref[...]
        st_ref[...] = h0_ref[...]

    xp_ref[SUB:SUB + C, :] = z_ref[:, 0:M_CONV_DIM]
    conv = cb_ref[...]
    for j in range(K):
        conv = conv + xp_ref[SUB - (K - 1) + j:SUB - (K - 1) + j + C, :] * cw_ref[j:j + 1, :]
    tail = xp_ref[SUB + T - (K - 1):SUB + T, :]
    xp_ref[SUB - (K - 1):SUB, :] = tail
    xbc = _silu(conv)

    lane = lax.broadcasted_iota(jnp.int32, (C, LANES), 1)
    ri = lax.broadcasted_iota(jnp.int32, (C, LANES), 0)
    dt = _softplus(dt_ref[...] + dtb_ref[...])
    dt = jnp.where((lane < M_HEADS) & (ri < T), dt, 0.0)
    a = dt * (-jnp.exp(alog_ref[...]))
    a_cs = _cumsum_rows(a)
    eye8 = _eye_rows(SUB, LANES)
    a_row = _exact_rows(eye8, a_cs)
    dt_row = _exact_rows(eye8, dt)
    a_last = a_cs[C - 1:C, :]
    w_end = jnp.exp(a_last - a_cs) * dt
    ea = jnp.exp(a_cs)
    e_last = jnp.exp(a_last)

    di = lax.broadcasted_iota(jnp.int32, (C, C), 0)
    dj = lax.broadcasted_iota(jnp.int32, (C, C), 1)
    causal = di >= dj
    lo = lane < M_HD
    rlo = lax.broadcasted_iota(jnp.int32, (LANES, LANES), 0) < M_HD

    ys = []
    heads_per_group = M_HEADS // M_GROUPS
    for g in range(M_GROUPS):
        bg = xbc[:, M_WIDTH + g * M_STATE:M_WIDTH + (g + 1) * M_STATE]
        cg = xbc[:, M_WIDTH + (M_GROUPS + g) * M_STATE:M_WIDTH + (M_GROUPS + g + 1) * M_STATE]
        cbm = _bdot_nt(cg, bg)
        for pp in range(heads_per_group // 2):
            p = g * (heads_per_group // 2) + pp
            xpair = xbc[:, p * LANES:(p + 1) * LANES]
            yh = []
            for e in range(2):
                h = 2 * p + e
                seg = a_cs[:, h:h + 1] - a_row[h:h + 1, :]
                lm = jnp.exp(jnp.where(causal, seg, NEG))
                m = cbm * lm * dt_row[h:h + 1, :]
                yh.append(_bdot(m, xpair))
            sp = st_ref[p]
            ycross = _bdot_nt(cg, sp) * jnp.where(lo, ea[:, 2 * p:2 * p + 1], ea[:, 2 * p + 1:2 * p + 2])
            y = jnp.where(lo, yh[0], yh[1]) + ycross + dvec_ref[:, p * LANES:(p + 1) * LANES] * xpair
            ys.append(y)
            xs = xpair * jnp.where(lo, w_end[:, 2 * p:2 * p + 1], w_end[:, 2 * p + 1:2 * p + 2])
            dstate = _bdot_tn(xs, bg)
            dec = jnp.where(rlo, e_last[:, 2 * p:2 * p + 1], e_last[:, 2 * p + 1:2 * p + 2])
            st_ref[p] = dec * sp + dstate

    my = jnp.concatenate(ys, axis=1)
    gated = my * _silu(z_ref[:, M_CONV_DIM:M_CONV_DIM + M_WIDTH])
    o_ref[...] = (gated * lax.rsqrt(jnp.mean(gated * gated, axis=-1, keepdims=True) + EPS) * ng_ref[...]).astype(o_ref.dtype)

    @pl.when(c == pl.num_programs(1) - 1)
    def _():
        cnew_ref[...] = xp_ref[SUB - (K - 1):SUB, :]
        hl_ref[...] = st_ref[...]


def _mamba(z, zcol, small, scol, cw, cb, dtb, alog, dvec, ng, cbuf, h0, *, B, n, C, T):
    nt = B * n * C
    kern = functools.partial(_mamba_kernel, C=C, T=T)
    np_ = M_HEADS // 2
    return pl.pallas_call(
        kern,
        grid=(B, n),
        in_specs=[
            pl.BlockSpec((C, 1536), lambda b, c: (b * n + c, zcol)),
            pl.BlockSpec((C, LANES), lambda b, c: (b * n + c, scol)),
            pl.BlockSpec((M_CONV, M_CONV_DIM), lambda b, c: (0, 0)),
            pl.BlockSpec((1, M_CONV_DIM), lambda b, c: (0, 0)),
            pl.BlockSpec((1, LANES), lambda b, c: (0, 0)),
            pl.BlockSpec((1, LANES), lambda b, c: (0, 0)),
            pl.BlockSpec((1, M_WIDTH), lambda b, c: (0, 0)),
            pl.BlockSpec((1, M_WIDTH), lambda b, c: (0, 0)),
            pl.BlockSpec((None, M_CONV - 1, M_CONV_DIM), lambda b, c: (b, 0, 0)),
            pl.BlockSpec((None, np_, LANES, LANES), lambda b, c: (b, 0, 0, 0)),
        ],
        out_specs=[
            pl.BlockSpec((C, M_WIDTH), lambda b, c: (b * n + c, 0)),
            pl.BlockSpec((None, M_CONV - 1, M_CONV_DIM), lambda b, c: (b, 0, 0)),
            pl.BlockSpec((None, np_, LANES, LANES), lambda b, c: (b, 0, 0, 0)),
        ],
        out_shape=[
            jax.ShapeDtypeStruct((nt, M_WIDTH), BF16),
            jax.ShapeDtypeStruct((B, M_CONV - 1, M_CONV_DIM), F32),
            jax.ShapeDtypeStruct((B, np_, LANES, LANES), F32),
        ],
        scratch_shapes=[pltpu.VMEM((SUB + C, M_CONV_DIM), F32), pltpu.VMEM((np_, LANES, LANES), F32)],
        compiler_params=_cparams(("arbitrary", "arbitrary")),
        name="mamba",
    )(z, small, cw, cb, dtb, alog, dvec, ng, cbuf, h0)


def _fox_qkv(z_ref, ff_ref, gq_ref, gk_ref, fb_ref, m64_ref):
    m64 = m64_ref[...]

    def headnorm(x, g):
        x2 = x * x
        hi = x2.astype(BF16)
        lo = (x2 - hi.astype(F32)).astype(BF16)
        ms = jnp.dot(hi, m64, preferred_element_type=F32) + jnp.dot(lo, m64, preferred_element_type=F32)
        return x * lax.rsqrt(ms + EPS) * g

    qs = headnorm(z_ref[:, 0:F_WIDTH], gq_ref[...]) * (F_DH ** -0.5)
    kn = headnorm(z_ref[:, F_WIDTH:2 * F_WIDTH], gk_ref[...])
    v = z_ref[:, 2 * F_WIDTH:3 * F_WIDTH]
    lane = lax.broadcasted_iota(jnp.int32, ff_ref.shape, 1)
    logf = jnp.where(lane < F_HEADS, _log_sigmoid(ff_ref[...] + fb_ref[...]), 0.0)
    return qs, kn, v, logf


def _foxprep_prompt_kernel(z_ref, ff_ref, gq_ref, gk_ref, fb_ref, m64_ref, exp_ref,
                           qt_ref, kb_ref, kft_ref, vft_ref, vt_ref, lft_ref, ccol_ref, crow_ref, carry_ref, *, tm):
    qs, kn, v, logf = _fox_qkv(z_ref, ff_ref, gq_ref, gk_ref, fb_ref, m64_ref)
    kb_ref[...] = kn.astype(BF16)
    eye8 = _eye_rows(SUB, LANES)
    lft_ref[...] = _exact_rows(eye8, logf)
    qs2 = qs * LOG2E
    for p in range(F_HEADS // 2):
        cols = slice(p * LANES, (p + 1) * LANES)
        qt_ref[p] = jnp.transpose(qs2[:, cols]).astype(BF16)
        kft_ref[cols, :] = jnp.transpose(kn[:, cols])
        vtp = jnp.transpose(v[:, cols])
        vft_ref[cols, :] = vtp
        vt_ref[p] = vtp.astype(BF16)

    @pl.when(pl.program_id(1) == 0)
    def _():
        carry_ref[...] = jnp.zeros_like(carry_ref)

    cs = _cumsum_rows(logf) + carry_ref[...]
    carry_ref[...] = cs[tm - 1:tm, :]
    cs2 = cs * LOG2E
    ccol_ref[...] = _exact_dot(cs2, exp_ref[...])
    crow = _exact_rows(eye8, cs2)
    for p in range(F_HEADS // 2):
        crow_ref[p] = crow[2 * p:2 * p + 2, :]


def _foxprep_prompt(z, small, gq, gk, fb, m64, expand, *, B, nb, tm):
    nt = B * nb * tm
    kern = functools.partial(_foxprep_prompt_kernel, tm=tm)
    zc = Z_FOX // 2048
    npair = F_HEADS // 2
    row = lambda b, i: (b * nb + i, 0)
    const = lambda b, i: (0, 0)
    return pl.pallas_call(
        kern,
        grid=(B, nb),
        in_specs=[
            pl.BlockSpec((tm, 2048), lambda b, i: (b * nb + i, zc)),
            pl.BlockSpec((tm, LANES), row),
            pl.BlockSpec((1, F_WIDTH), const),
            pl.BlockSpec((1, F_WIDTH), const),
            pl.BlockSpec((1, LANES), const),
            pl.BlockSpec((F_WIDTH, F_WIDTH), const),
            pl.BlockSpec((LANES, F_HEADS * LANES), const),
        ],
        out_specs=[
            pl.BlockSpec((None, npair, None, LANES, tm), lambda b, i: (b, 0, i, 0, 0)),
            pl.BlockSpec((tm, F_WIDTH), row),
            pl.BlockSpec((None, F_WIDTH, tm), lambda b, i: (b, 0, i)),
            pl.BlockSpec((None, F_WIDTH, tm), lambda b, i: (b, 0, i)),
            pl.BlockSpec((None, npair, None, LANES, tm), lambda b, i: (b, 0, i, 0, 0)),
            pl.BlockSpec((None, F_HEADS, tm), lambda b, i: (b, 0, i)),
            pl.BlockSpec((tm, F_HEADS * LANES), row),
            pl.BlockSpec((None, None, npair, 2, tm), lambda b, i: (b, i, 0, 0, 0)),
        ],
        out_shape=[
            jax.ShapeDtypeStruct((B, npair, nb, LANES, tm), BF16),
            jax.ShapeDtypeStruct((nt, F_WIDTH), BF16),
            jax.ShapeDtypeStruct((B, F_WIDTH, nb * tm), F32),
            jax.ShapeDtypeStruct((B, F_WIDTH, nb * tm), F32),
            jax.ShapeDtypeStruct((B, npair, nb, LANES, tm), BF16),
            jax.ShapeDtypeStruct((B, F_HEADS, nb * tm), F32),
            jax.ShapeDtypeStruct((nt, F_HEADS * LANES), F32),
            jax.ShapeDtypeStruct((B, nb, npair, 2, tm), F32),
        ],
        scratch_shapes=[pltpu.VMEM((1, LANES), F32)],
        compiler_params=_cparams(("arbitrary", "arbitrary")),
        name="foxprep_prompt",
    )(z, small, gq, gk, fb, m64, expand)


def _foxprep_sample_kernel(z_ref, ff_ref, gq_ref, gk_ref, fb_ref, m64_ref, qs_ref, kf_ref, vf_ref, lf_ref):
    qs, kn, v, logf = _fox_qkv(z_ref, ff_ref, gq_ref, gk_ref, fb_ref, m64_ref)
    qs_ref[...] = qs
    kf_ref[...] = kn
    vf_ref[...] = v
    lf_ref[...] = logf


def _foxprep_sample(z, small, gq, gk, fb, m64):
    nt = z.shape[0]
    zc = Z_FOX // 2048
    row = lambda i: (i, 0)
    const = lambda i: (0, 0)
    return pl.pallas_call(
        _foxprep_sample_kernel,
        grid=(1,),
        in_specs=[
            pl.BlockSpec((nt, 2048), lambda i: (i, zc)),
            pl.BlockSpec((nt, LANES), row),
            pl.BlockSpec((1, F_WIDTH), const),
            pl.BlockSpec((1, F_WIDTH), const),
            pl.BlockSpec((1, LANES), const),
            pl.BlockSpec((F_WIDTH, F_WIDTH), const),
        ],
        out_specs=[
            pl.BlockSpec((nt, F_WIDTH), row),
            pl.BlockSpec((nt, F_WIDTH), row),
            pl.BlockSpec((nt, F_WIDTH), row),
            pl.BlockSpec((nt, LANES), row),
        ],
        out_shape=[
            jax.ShapeDtypeStruct((nt, F_WIDTH), F32),
            jax.ShapeDtypeStruct((nt, F_WIDTH), F32),
            jax.ShapeDtypeStruct((nt, F_WIDTH), F32),
            jax.ShapeDtypeStruct((nt, LANES), F32),
        ],
        compiler_params=_cparams(("arbitrary",)),
        name="foxprep_sample",
    )(z, small, gq, gk, fb, m64)


def _foxattn_kernel(qt_ref, k_ref, vt_ref, cq_ref, cke_ref, cko_ref, fg_ref, o_ref, m_ref, l_ref, acc_ref, s_ref,
                    *, tq):
    qi = pl.program_id(2)
    qt = qt_ref[...]
    rlo = lax.broadcasted_iota(jnp.int32, (LANES, tq), 0) < F_DH
    zero = jnp.zeros_like(qt)
    q2t = jnp.concatenate([jnp.where(rlo, qt, zero), jnp.where(rlo, zero, qt)], axis=1)
    cq2 = jnp.concatenate([cq_ref[0:1, :], cq_ref[1:2, :]], axis=1)
    m_ref[...] = jnp.full_like(m_ref, NEG)
    l_ref[...] = jnp.zeros_like(l_ref)
    acc_ref[...] = jnp.zeros_like(acc_ref)
    kpos = lax.broadcasted_iota(jnp.int32, (tq, 2 * tq), 0)
    qpos = lax.broadcasted_iota(jnp.int32, (tq, 2 * tq), 1) % tq
    causal = kpos <= qpos

    strips = list(range(0, 2 * tq, STRIP))

    def keys(j):
        return k_ref[pl.ds(pl.multiple_of(j * tq, tq), tq), :]

    def qk(k, c0):
        return jnp.dot(k, q2t[:, c0:c0 + STRIP], preferred_element_type=F32)

    def block(j, masked):
        slot = j % 2
        start = pl.multiple_of(j * tq, tq)
        vt = vt_ref[j]
        cks = (cke_ref[pl.ds(start, tq), :], cko_ref[pl.ds(start, tq), :])
        k_next = None if masked else keys(j + 1)
        m_all, l_all, acc_all = m_ref[...], l_ref[...], acc_ref[...]
        m_out, l_out, acc_out = [], [], []
        for c0 in strips:
            cols = slice(c0, c0 + STRIP)
            st = s_ref[slot, :, cols]
            nxt = None if masked else qk(k_next, c0)
            u = st - jnp.concatenate([cks[c0 // tq]] * (STRIP // LANES), axis=1)
            if masked:
                u = jnp.where(causal[:, cols], u, NEG)
            cq = cq2[:, cols]
            m_prev = m_all[:, cols]
            m_new = jnp.maximum(m_prev, jnp.max(u, axis=0, keepdims=True) + cq)
            alpha = jnp.exp2(m_prev - m_new)
            p = jnp.exp2(u + (cq - m_new))
            l_out.append(alpha * l_all[:, cols] + jnp.sum(p, axis=0, keepdims=True))
            acc_out.append(alpha * acc_all[:, cols] + jnp.dot(vt, p.astype(BF16), preferred_element_type=F32))
            m_out.append(m_new)
            if not masked:
                s_ref[1 - slot, :, cols] = nxt
        m_ref[...] = jnp.concatenate(m_out, axis=1)
        l_ref[...] = jnp.concatenate(l_out, axis=1)
        acc_ref[...] = jnp.concatenate(acc_out, axis=1)

    k0 = keys(0)
    for c0 in strips:
        s_ref[0, :, c0:c0 + STRIP] = qk(k0, c0)

    def body(j, carry):
        block(j, False)
        return carry

    lax.fori_loop(0, qi, body, 0)
    block(qi, True)
    on = acc_ref[...] / l_ref[...]
    ot = jnp.concatenate([on[0:F_DH, 0:tq], on[F_DH:2 * F_DH, tq:2 * tq]], axis=0)
    o_ref[...] = (jnp.transpose(ot) * _silu(fg_ref[...])).astype(o_ref.dtype)


def _foxattn(qt, kb, vt, ccol, crow, z, *, B, L, tq):
    nq = L // tq
    npair = F_HEADS // 2
    fgc = (Z_FOX + 3 * F_WIDTH) // LANES
    kern = functools.partial(_foxattn_kernel, tq=tq)
    return pl.pallas_call(
        kern,
        grid=(B, npair, nq),
        in_specs=[
            pl.BlockSpec((None, None, None, LANES, tq), lambda b, p, i: (b, p, i, 0, 0)),
            pl.BlockSpec((L, LANES), lambda b, p, i: (b, p)),
            pl.BlockSpec((None, None, nq, LANES, tq), lambda b, p, i: (b, p, 0, 0, 0)),
            pl.BlockSpec((None, None, None, 2, tq), lambda b, p, i: (b, i, p, 0, 0)),
            pl.BlockSpec((L, LANES), lambda b, p, i: (b, 2 * p)),
            pl.BlockSpec((L, LANES), lambda b, p, i: (b, 2 * p + 1)),
            pl.BlockSpec((tq, LANES), lambda b, p, i: (b * nq + i, fgc + p)),
        ],
        out_specs=pl.BlockSpec((tq, LANES), lambda b, p, i: (b * nq + i, p)),
        out_shape=jax.ShapeDtypeStruct((B * L, F_WIDTH), BF16),
        scratch_shapes=[pltpu.VMEM((1, 2 * tq), F32), pltpu.VMEM((1, 2 * tq), F32), pltpu.VMEM((LANES, 2 * tq), F32),
                        pltpu.VMEM((2, tq, 2 * tq), F32)],
        compiler_params=_cparams(("arbitrary", "arbitrary", "arbitrary")),
        name="foxattn",
    )(qt, kb, vt, crow, ccol, ccol, z)


def _pagecum_kernel(pt_ref, lf_hbm, ck_ref, tot_ref, g_ref, sem, *, n_pages, page_off):
    b = pl.program_id(0)

    def copy(bb, pg, slot):
        return pltpu.make_async_copy(lf_hbm.at[pt_ref[bb, pg] + page_off], g_ref.at[slot, pg], sem.at[slot])

    def start_all(bb, slot):
        def body(pg, c):
            copy(bb, pg, slot).start()
            return c
        lax.fori_loop(0, n_pages, body, 0)

    @pl.when(b == 0)
    def _():
        start_all(b, 0)

    @pl.when(b + 1 < pl.num_programs(0))
    def _():
        start_all(b + 1, (b + 1) % 2)

    slot = b % 2

    def wait(pg, c):
        copy(b, pg, slot).wait()
        return c

    lax.fori_loop(0, n_pages, wait, 0)

    rows = n_pages * F_HEADS
    w = g_ref[slot].reshape(rows, PAGE)
    row = lax.broadcasted_iota(jnp.int32, (rows, PAGE), 0)
    tri = (lax.broadcasted_iota(jnp.int32, (PAGE, PAGE), 0) <= lax.broadcasted_iota(jnp.int32, (PAGE, PAGE), 1)).astype(BF16)
    ones = jnp.ones((PAGE, PAGE), BF16)
    parts = _split3(w)
    w = sum(jnp.dot(h, tri, preferred_element_type=F32) for h in parts)
    tot = sum(jnp.dot(h, ones, preferred_element_type=F32) for h in parts)
    inc = tot
    s = F_HEADS
    while s < rows:
        inc = inc + jnp.where(row >= s, pltpu.roll(inc, s, 0), 0.0)
        s *= 2
    ck_ref[...] = (w + (inc - tot)).reshape(n_pages, F_HEADS, PAGE)
    last = inc[rows - F_HEADS:rows, :]
    eye = (lax.broadcasted_iota(jnp.int32, (F_HEADS, LANES), 0) == lax.broadcasted_iota(jnp.int32, (F_HEADS, LANES), 1))
    tot_ref[...] = jnp.sum(jnp.where(eye, last, 0.0), axis=0, keepdims=True)


def _pagecum(page_table, lf_cache, *, page_off):
    bd, n_pages = page_table.shape
    kern = functools.partial(_pagecum_kernel, n_pages=n_pages, page_off=page_off)
    return pl.pallas_call(
        kern,
        grid_spec=pltpu.PrefetchScalarGridSpec(
            num_scalar_prefetch=1,
            grid=(bd,),
            in_specs=[pl.BlockSpec(memory_space=pl.ANY)],
            out_specs=[
                pl.BlockSpec((None, n_pages, F_HEADS, PAGE), lambda b, pt: (b, 0, 0, 0)),
                pl.BlockSpec((None, 1, LANES), lambda b, pt: (b, 0, 0)),
            ],
            scratch_shapes=[pltpu.VMEM((2, n_pages, F_HEADS, PAGE), F32), pltpu.SemaphoreType.DMA((2,))],
        ),
        out_shape=[
            jax.ShapeDtypeStruct((bd, n_pages, F_HEADS, PAGE), F32),
            jax.ShapeDtypeStruct((bd, 1, LANES), F32),
        ],
        compiler_params=_cparams(("arbitrary",)),
        name="pagecum",
    )(page_table, lf_cache)


def _decode_kernel(pt_ref, q_ref, kn_ref, vn_ref, lfn_ref, tot_ref, ck_ref, fg_ref, k_hbm, v_hbm,
                   o_ref, kbuf, vbuf, sem, qbd_ref, cq_ref, cn_ref, m_ref, l_ref, acc_ref,
                   *, T, ppb, ns, page_off):
    b = pl.program_id(0)
    s = pl.program_id(1)
    t = b * ns + s
    nt = pl.num_programs(0) * ns
    R = T * F_HEADS

    def copies(bb, ss, slot):
        out = []
        for i in range(ppb):
            page = pt_ref[bb, ss * ppb + i] + page_off
            out.append(pltpu.make_async_copy(k_hbm.at[page], kbuf.at[slot, i], sem.at[0, slot]))
            out.append(pltpu.make_async_copy(v_hbm.at[page], vbuf.at[slot, i], sem.at[1, slot]))
        return out

    @pl.when(t == 0)
    def _():
        for cp in copies(b, s, 0):
            cp.start()

    @pl.when(t + 1 < nt)
    def _():
        t1 = t + 1
        for cp in copies(t1 // ns, t1 % ns, t1 % 2):
            cp.start()

    rl = lax.broadcasted_iota(jnp.int32, (SUB, F_WIDTH), 1) // F_DH
    rh = lax.broadcasted_iota(jnp.int32, (SUB, F_WIDTH), 0)
    hmask = rl == rh

    @pl.when(s == 0)
    def _():
        q = q_ref[...].astype(F32)
        qbd_ref[...] = jnp.concatenate(
            [jnp.where(hmask, jnp.broadcast_to(q[tt:tt + 1, :], (SUB, F_WIDTH)), 0.0) for tt in range(T)],
            axis=0).astype(BF16)
        cn = tot_ref[...] + _cumsum_rows(lfn_ref[...])
        cn_ref[...] = cn
        eye = (lax.broadcasted_iota(jnp.int32, (SUB, LANES), 0) == lax.broadcasted_iota(jnp.int32, (SUB, LANES), 1))
        cq_ref[...] = jnp.concatenate(
            [jnp.sum(jnp.where(eye, jnp.broadcast_to(cn[tt:tt + 1, :], (SUB, LANES)), 0.0), axis=-1, keepdims=True)
             for tt in range(T)], axis=0)
        m_ref[...] = jnp.full_like(m_ref, NEG)
        l_ref[...] = jnp.zeros_like(l_ref)
        acc_ref[...] = jnp.zeros_like(acc_ref)

    slot = t % 2
    for cp in copies(b, s, slot):
        cp.wait()

    def update(sc, v):
        m_prev = m_ref[...]
        m_new = jnp.maximum(m_prev, jnp.max(sc, axis=-1, keepdims=True))
        alpha = jnp.exp(m_prev - m_new)
        p = jnp.exp(sc - m_new)
        l_ref[...] = alpha * l_ref[...] + jnp.sum(p, axis=-1, keepdims=True)
        acc_ref[...] = alpha * acc_ref[...] + jnp.dot(p.astype(BF16), v, preferred_element_type=F32)
        m_ref[...] = m_new

    qbd = qbd_ref[...]
    cq = cq_ref[...]
    sc = jnp.concatenate(
        [jnp.dot(qbd, kbuf[slot, i].astype(BF16), preferred_element_type=F32) for i in range(ppb)], axis=1)
    ck = jnp.concatenate([ck_ref[i] for i in range(ppb)], axis=1)
    sc = sc + cq - jnp.concatenate([ck] * T, axis=0)
    m_prev = m_ref[...]
    m_new = jnp.maximum(m_prev, jnp.max(sc, axis=-1, keepdims=True))
    alpha = jnp.exp(m_prev - m_new)
    pf = jnp.exp(sc - m_new)
    l_ref[...] = alpha * l_ref[...] + jnp.sum(pf, axis=-1, keepdims=True)
    p = pf.astype(BF16)
    pv =lax.dot_general(p[:, 0:PAGE], vbuf[slot, 0].astype(BF16), NT_DIMS, preferred_element_type=F32)
    for i in range(1, ppb):
        pv = pv + lax.dot_general(p[:, i * PAGE:(i + 1) * PAGE], vbuf[slot, i].astype(BF16), NT_DIMS,
                                  preferred_element_type=F32)
    acc_ref[...] = alpha * acc_ref[...] + pv
    m_ref[...] = m_new

    @pl.when(s == ns - 1)
    def _():
        cn = cn_ref[...]
        cn_pad = jnp.concatenate([cn, jnp.zeros((PAGE - SUB, LANES), F32)], axis=0)
        cn_row = _exact_rows(_eye_rows(SUB, LANES), cn_pad)
        sn = lax.dot_general(qbd, kn_ref[...].astype(BF16), NT_DIMS, preferred_element_type=F32)
        sn = sn + cq - jnp.concatenate([cn_row] * T, axis=0)
        rt = lax.broadcasted_iota(jnp.int32, (R, PAGE), 0) // F_HEADS
        ct = lax.broadcasted_iota(jnp.int32, (R, PAGE), 1)
        sn = jnp.where((ct <= rt) & (ct < T), sn, NEG)
        update(sn, vn_ref[...].astype(BF16))
        on = acc_ref[...] / l_ref[...]
        for tt in range(T):
            blk = jnp.where(hmask, on[tt * F_HEADS:(tt + 1) * F_HEADS, :], 0.0)
            o_ref[tt:tt + 1, :] = jnp.sum(blk, axis=0, keepdims=True) * _silu(fg_ref[tt:tt + 1, :])
        o_ref[T:SUB, :] = jnp.zeros((SUB - T, F_WIDTH), F32)


def _decode(page_table, qb, kn_pad, vn_pad, lf, tot, ck, z, k_cache, v_cache, *, T, page_off, ppb):
    bd, n_pages = page_table.shape
    ns = n_pages // ppb
    R = T * F_HEADS
    fgc = (Z_FOX + 3 * F_WIDTH) // F_WIDTH
    kern = functools.partial(_decode_kernel, T=T, ppb=ppb, ns=ns, page_off=page_off)
    return pl.pallas_call(
        kern,
        grid_spec=pltpu.PrefetchScalarGridSpec(
            num_scalar_prefetch=1,
            grid=(bd, ns),
            in_specs=[
                pl.BlockSpec((TPAD, F_WIDTH), lambda b, s, pt: (b, 0)),
                pl.BlockSpec((None, PAGE, F_WIDTH), lambda b, s, pt: (b, 0, 0)),
                pl.BlockSpec((None, PAGE, F_WIDTH), lambda b, s, pt: (b, 0, 0)),
                pl.BlockSpec((TPAD, LANES), lambda b, s, pt: (b, 0)),
                pl.BlockSpec((None, 1, LANES), lambda b, s, pt: (b, 0, 0)),
                pl.BlockSpec((None, ppb, F_HEADS, PAGE), lambda b, s, pt: (b, s, 0, 0)),
                pl.BlockSpec((TPAD, F_WIDTH), lambda b, s, pt: (b, fgc)),
                pl.BlockSpec(memory_space=pl.ANY),
                pl.BlockSpec(memory_space=pl.ANY),
            ],
            out_specs=pl.BlockSpec((TPAD, F_WIDTH), lambda b, s, pt: (b, 0)),
            scratch_shapes=[
                pltpu.VMEM((2, ppb, F_WIDTH, PAGE), F32),
                pltpu.VMEM((2, ppb, F_WIDTH, PAGE), F32),
                pltpu.SemaphoreType.DMA((2, 2)),
                pltpu.VMEM((R, F_WIDTH), BF16),
                pltpu.VMEM((R, 1), F32),
                pltpu.VMEM((SUB, LANES), F32),
                pltpu.VMEM((R, 1), F32),
                pltpu.VMEM((R, 1), F32),
                pltpu.VMEM((R, F_WIDTH), F32),
            ],
        ),
        out_shape=jax.ShapeDtypeStruct((bd * TPAD, F_WIDTH), F32),
        compiler_params=_cparams(("arbitrary", "arbitrary")),
        name="decode",
    )(page_table, qb, kn_pad, vn_pad, lf, tot, ck, z, k_cache, v_cache)


def _merge_kernel(x_ref, r_ref, f_ref, m_ref, gate_ref, wr_ref, wf_ref, wm_ref, wo_ref, y_ref):
    br = _bdot(r_ref[...], wr_ref[...])
    bf = _bdot(f_ref[...], wf_ref[...])
    bm = _bdot(m_ref[...], wm_ref[...])
    merged = (jax.nn.sigmoid(gate_ref[:, 0:D_MODEL]) * br
              + jax.nn.sigmoid(gate_ref[:, D_MODEL:2 * D_MODEL]) * bf
              + jax.nn.sigmoid(gate_ref[:, 2 * D_MODEL:3 * D_MODEL]) * bm)
    y_ref[...] = x_ref[...] + _bdot(merged, wo_ref[...])


def _merge(x, r, f, m, z, wr, wf, wm, wo):
    nt = x.shape[0]
    tm = min(nt, 512)
    row = lambda i: (i, 0)
    full = lambda i: (0, 0)
    return pl.pallas_call(
        _merge_kernel,
        grid=(nt // tm,),
        in_specs=[
            pl.BlockSpec((tm, D_MODEL), row),
            pl.BlockSpec((tm, 512), row),
            pl.BlockSpec((tm, 512), row),
            pl.BlockSpec((tm, 512), row),
            pl.BlockSpec((tm, 3 * D_MODEL), lambda i: (i, Z_GATE // (3 * D_MODEL))),
            pl.BlockSpec((512, D_MODEL), full),
            pl.BlockSpec((512, D_MODEL), full),
            pl.BlockSpec((512, D_MODEL), full),
            pl.BlockSpec((D_MODEL, D_MODEL), full),
        ],
        out_specs=pl.BlockSpec((tm, D_MODEL), row),
        out_shape=jax.ShapeDtypeStruct((nt, D_MODEL), F32),
        compiler_params=_cparams(("arbitrary",)),
        name="merge",
    )(x, r, f, m, z, wr, wf, wm, wo)


def _rope_tables(pos):
    inv = ROPE_BASE ** (-jnp.arange(0, R_DK, 2, dtype=F32) / R_DK)
    ang = pos.astype(F32)[:, None] * inv[None, :]
    cos, sin = jnp.cos(ang), jnp.sin(ang)
    cos_h = jnp.concatenate([cos, cos], axis=1)
    sin_h = jnp.concatenate([-sin, sin], axis=1)
    return jnp.tile(cos_h, (1, R_HEADS)), jnp.tile(sin_h, (1, R_HEADS))


def _pad_lanes(v, off=0, width=LANES):
    return jnp.zeros((1, width), F32).at[0, off:off + v.shape[0]].set(v)


def _layer_weights(l, norm_g, w_in, r_norm_g, f_qnorm_g, f_knorm_g, f_bias, m_conv_w, m_conv_b,
                   m_dt_bias, m_A_log, m_D, m_norm_g, w_br_r, w_br_f, w_br_m, w_out):
    w = w_in[l]
    w_main = jnp.concatenate(
        [w[:, 0:1536], w[:, 3592:4616], w[:, 4624:5136], w[:, 5136:8208], w[:, 1536:3072], w[:, 3080:3592]],
        axis=1).astype(BF16)
    zpad = jnp.zeros((D_MODEL, LANES - F_HEADS), F32)
    w_small = jnp.concatenate([w[:, 3072:3080], zpad, w[:, 4616:4624], zpad], axis=1).astype(BF16)
    return dict(
        g=norm_g[l][None, :], w_main=w_main, w_small=w_small,
        gn=r_norm_g[l][None, :],
        gq=jnp.tile(f_qnorm_g[l], F_HEADS)[None, :], gk=jnp.tile(f_knorm_g[l], F_HEADS)[None, :],
        fb=_pad_lanes(f_bias[l]),
        cw=m_conv_w[l], cb=m_conv_b[l][None, :],
        dtb=_pad_lanes(m_dt_bias[l]), alog=_pad_lanes(m_A_log[l]),
        dvec=jnp.repeat(m_D[l], M_HD)[None, :], ng=m_norm_g[l][None, :],
        wr=w_br_r[l].astype(BF16), wf=w_br_f[l].astype(BF16), wm=w_br_m[l].astype(BF16), wo=w_out[l].astype(BF16),
    )


def _consts():
    i = np.arange(F_WIDTH)
    m64 = jnp.asarray((i[:, None] // F_DH == i[None, :] // F_DH).astype(np.float32) / F_DH, dtype=BF16)
    j = np.arange(F_HEADS * LANES)
    expand = jnp.asarray((np.arange(LANES)[:, None] == j[None, :] // LANES).astype(np.float32), dtype=BF16)
    return m64, expand


def kernel(x_prompt, x_sample, cache_fox_k, cache_fox_v, cache_fox_logf, page_table, state_ret, state_conv, state_ssm, norm_g, w_in, r_norm_g, f_qnorm_g, f_knorm_g, f_bias, m_conv_w, m_conv_b, m_dt_bias, m_A_log, m_D, m_norm_g, w_br_r, w_br_f, w_br_m, w_out):
    B, L, D = x_prompt.shape
    Bd, T, _ = x_sample.shape
    depth = w_in.shape[0]
    n_pool = cache_fox_k.shape[1]
    n_pages = page_table.shape[1]
    past = n_pages * PAGE
    C = CHUNK
    n = L // C
    tq = min(L, 512)
    ppb = min(n_pages, 16)

    m64, expand = _consts()
    cos_p, sin_p = _rope_tables(jnp.arange(L))
    cos_s, sin_s = _rope_tables(past + jnp.arange(C))
    kc = jnp.transpose(cache_fox_k, (0, 1, 3, 4, 2)).reshape(depth * n_pool, F_WIDTH, PAGE)
    vc = jnp.transpose(cache_fox_v, (0, 1, 3, 4, 2)).reshape(depth * n_pool, F_WIDTH, PAGE)
    lc = jnp.transpose(cache_fox_logf, (0, 1, 3, 2)).reshape(depth * n_pool, F_HEADS, PAGE)

    hp = x_prompt.reshape(B * L, D)
    hs = jnp.pad(x_sample, ((0, 0), (0, TPAD - T), (0, 0))).reshape(Bd * TPAD, D)
    zeros_ret = jnp.zeros((B, 2, LANES, LANES), F32)
    zeros_conv = jnp.zeros((B, M_CONV - 1, M_CONV_DIM), F32)
    zeros_ssm = jnp.zeros((B, M_HEADS // 2, LANES, LANES), F32)

    def pad_chunk(a):
        w = a.shape[1]
        return jnp.pad(a.reshape(Bd, TPAD, w), ((0, 0), (0, C - TPAD), (0, 0))).reshape(Bd * C, w)

    def unpad_chunk(a):
        w = a.shape[1]
        return a.reshape(Bd, C, w)[:, :TPAD].reshape(Bd * TPAD, w)

    outs_p, outs_s = [], []
    for l in range(depth):
        W = _layer_weights(l, norm_g, w_in, r_norm_g, f_qnorm_g, f_knorm_g, f_bias, m_conv_w, m_conv_b,
                           m_dt_bias, m_A_log, m_D, m_norm_g, w_br_r, w_br_f, w_br_m, w_out)
        z, small = _inproj(hp, W["g"], W["w_main"], W["w_small"])
        r_out, ret_p = _retention(z, Z_RET // 1536, cos_p, sin_p, W["gn"], zeros_ret, B=B, n=n, C=C, T=C)
        qt, kb, kft, vft, vt, lft, ccol, crow = _foxprep_prompt(
            z, small, W["gq"], W["gk"], W["fb"], m64, expand, B=B, nb=L // tq, tm=tq)
        f_out = _foxattn(qt, kb, vt, ccol, crow, z, B=B, L=L, tq=tq)
        m_out, conv_p, ssm_p = _mamba(z, Z_MAM // 1536, small, 1, W["cw"], W["cb"], W["dtb"], W["alog"],
                                      W["dvec"], W["ng"], zeros_conv, zeros_ssm, B=B, n=n, C=C, T=C)
        hp = _merge(hp, r_out, f_out, m_out, z, W["wr"], W["wf"], W["wm"], W["wo"])
        outs_p.append((kft, vft, lft, ret_p, conv_p, ssm_p))

        zs, smalls = _inproj(hs, W["g"], W["w_main"], W["w_small"])
        r_pad, ret_s = _retention(pad_chunk(zs[:, Z_RET:Z_RET + 1536]), 0, cos_s, sin_s, W["gn"],
                                  state_ret[l].reshape(Bd, 2, LANES, LANES), B=Bd, n=1, C=C, T=T)
        qs, kf, vf, lf = _foxprep_sample(zs, smalls, W["gq"], W["gk"], W["fb"], m64)
        ck, tot = _pagecum(page_table, lc, page_off=l * n_pool)
        f_out = _decode(page_table, qs, pad_chunk(kf).reshape(Bd, C, F_WIDTH), pad_chunk(vf).reshape(Bd, C, F_WIDTH),
                        lf, tot, ck, zs, kc, vc, T=T, page_off=l * n_pool, ppb=ppb)
        m_pad, conv_s, ssm_s = _mamba(pad_chunk(zs[:, Z_MAM:Z_MAM + 1536]), 0, pad_chunk(smalls[:, LANES:]), 0,
                                      W["cw"], W["cb"], W["dtb"], W["alog"], W["dvec"], W["ng"],
                                      state_conv[l], state_ssm[l].reshape(Bd, M_HEADS // 2, LANES, LANES),
                                      B=Bd, n=1, C=C, T=T)
        hs = _merge(hs, unpad_chunk(r_pad), f_out, unpad_chunk(m_pad), zs, W["wr"], W["wf"], W["wm"], W["wo"])
        sel = lambda a: a.reshape(Bd, TPAD, -1)[:, :T]
        outs_s.append((sel(kf), sel(vf), sel(lf)[:, :, :F_HEADS], ret_s, conv_s, ssm_s))

    stk = lambda outs, i: jnp.stack([o[i] for o in outs])
    return (
        hp.reshape(B, L, D),
        hs.reshape(Bd, TPAD, D)[:, :T],
        jnp.transpose(stk(outs_p, 0).reshape(depth, B, F_HEADS, F_DH, L), (0, 1, 4, 2, 3)),
        jnp.transpose(stk(outs_p, 1).reshape(depth, B, F_HEADS, F_DH, L), (0, 1, 4, 2, 3)),
        jnp.transpose(stk(outs_p, 2), (0, 1, 3, 2)),
        stk(outs_p, 3).reshape(depth, B, R_HEADS, R_DK, R_DV),
        stk(outs_p, 4),
        stk(outs_p, 5).reshape(depth, B, M_HEADS, M_HD, M_STATE),
        stk(outs_s, 0).reshape(depth, Bd, T, F_HEADS, F_DH),
        stk(outs_s, 1).reshape(depth, Bd, T, F_HEADS, F_DH),
        stk(outs_s, 2),
        stk(outs_s, 3).reshape(depth, Bd, R_HEADS, R_DK, R_DV),
        stk(outs_s, 4),
        stk(outs_s, 5).reshape(depth, Bd, M_HEADS, M_HD, M_STATE),
    )
```

```python
import functools
import math

import numpy as np
import jax
import jax.numpy as jnp
from jax import lax
from jax.experimental import pallas as pl
from jax.experimental.pallas import tpu as pltpu

F32 = jnp.float32
BF16 = jnp.bfloat16

D_MODEL = 1024
PAGE = 128
R_HEADS, R_DK, R_DV = 4, 64, 128
F_HEADS, F_DH = 8, 64
F_WIDTH = F_HEADS * F_DH
M_HEADS, M_HD, M_GROUPS, M_STATE, M_CONV = 8, 64, 2, 128, 4
M_WIDTH = M_HEADS * M_HD
M_CONV_DIM = M_WIDTH + 2 * M_GROUPS * M_STATE
CHUNK = 128
ROPE_BASE = 10000.0
EPS = 1e-6
NEG = -1e30
LOG2E = 1.4426950408889634

Z_RET, Z_MAM, Z_GATE, Z_FOX = 0, 1536, 3072, 6144
Z_WIDTH = 8192
SMALL_W = 256
LANES = 128
SUB = 8
TPAD = 8
STRIP = 256
SEQS = 2

VMEM_LIMIT = 56 * 1024 * 1024

NT_DIMS = (((1,), (1,)), ((), ()))
TN_DIMS = (((0,), (0,)), ((), ()))

LOG_G = [float(np.log1p(-np.exp2(-5.0 - h))) for h in range(R_HEADS)]


def _cparams(sem):
    return pltpu.CompilerParams(dimension_semantics=sem, vmem_limit_bytes=VMEM_LIMIT)


def _silu(x):
    return x * jax.nn.sigmoid(x)


def _softplus(x):
    return jnp.maximum(x, 0.0) + jnp.log1p(jnp.exp(-jnp.abs(x)))


def _log_sigmoid(x):
    return jnp.minimum(x, 0.0) - jnp.log1p(jnp.exp(-jnp.abs(x)))


def _bdot(a, b):
    return jnp.dot(a.astype(BF16), b.astype(BF16), preferred_element_type=F32)


def _bdot_nt(a, b):
    return lax.dot_general(a.astype(BF16), b.astype(BF16), NT_DIMS, preferred_element_type=F32)


def _bdot_tn(a, b):
    return lax.dot_general(a.astype(BF16), b.astype(BF16), TN_DIMS, preferred_element_type=F32)


def _split3(x):
    h1 = x.astype(BF16)
    r1 = x - h1.astype(F32)
    h2 = r1.astype(BF16)
    h3 = (r1 - h2.astype(F32)).astype(BF16)
    return h1, h2, h3


def _exact_dot(x, sel):
    h1, h2, h3 = _split3(x)
    d = lambda a: jnp.dot(a, sel, preferred_element_type=F32)
    return d(h1) + d(h2) + d(h3)


def _exact_rows(sel, x):
    h1, h2, h3 = _split3(x)
    d = lambda a: lax.dot_general(sel, a, NT_DIMS, preferred_element_type=F32)
    return d(h1) + d(h2) + d(h3)


def _cumsum_rows(x):
    n = x.shape[0]
    i = lax.broadcasted_iota(jnp.int32, x.shape, 0)
    s = 1
    while s < n:
        x = x + jnp.where(i >= s, pltpu.roll(x, s, 0), 0.0)
        s *= 2
    return x


def _eye_rows(r, k):
    return (lax.broadcasted_iota(jnp.int32, (r, k), 0) == lax.broadcasted_iota(jnp.int32, (r, k), 1)).astype(BF16)


def _inproj_kernel(x_ref, g_ref, w_ref, ws_ref, z_ref, s_ref, h_ref):
    @pl.when(pl.program_id(1) == 0)
    def _():
        x = x_ref[...]
        r = x * lax.rsqrt(jnp.mean(x * x, axis=-1, keepdims=True) + EPS)
        h = (r * g_ref[...]).astype(BF16)
        h_ref[...] = h
        s_ref[...] = jnp.dot(h, ws_ref[...], preferred_element_type=F32)

    z_ref[...] = jnp.dot(h_ref[...], w_ref[...], preferred_element_type=F32)


def _inproj(x, g, w_main, w_small):
    nt = x.shape[0]
    tm = next((t for t in (2048, 1024, 512, 256) if nt % t == 0), nt)
    tn = 1024
    return pl.pallas_call(
        _inproj_kernel,
        grid=(nt // tm, Z_WIDTH // tn),
        in_specs=[
            pl.BlockSpec((tm, D_MODEL), lambda i, j: (i, 0)),
            pl.BlockSpec((1, D_MODEL), lambda i, j: (0, 0)),
            pl.BlockSpec((D_MODEL, tn), lambda i, j: (0, j)),
            pl.BlockSpec((D_MODEL, SMALL_W), lambda i, j: (0, 0)),
        ],
        out_specs=[
            pl.BlockSpec((tm, tn), lambda i, j: (i, j)),
            pl.BlockSpec((tm, SMALL_W), lambda i, j: (i, 0)),
        ],
        out_shape=[jax.ShapeDtypeStruct((nt, Z_WIDTH), F32), jax.ShapeDtypeStruct((nt, SMALL_W), F32)],
        scratch_shapes=[pltpu.VMEM((tm, D_MODEL), BF16)],
        compiler_params=_cparams(("arbitrary", "arbitrary")),
        name="inproj",
    )(x, g, w_main, w_small)


def _ret_kernel(z_ref, cos_ref, sin_ref, gn_ref, s0_ref, o_ref, sl_ref, st_ref, *zp, C, T):
    c = pl.program_id(1)
    R = z_ref.shape[1]

    @pl.when(c == 0)
    def _():
        st_ref[...] = s0_ref[...]

    if R < C:
        src_ref, = zp

        @pl.when((pl.program_id(0) == 0) & (c == 0))
        def _():
            src_ref[...] = jnp.zeros_like(src_ref)

        src_ref[:, 0:R, :] = z_ref[...]
    else:
        src_ref = z_ref

    lane2 = lax.broadcasted_iota(jnp.int32, (C, 2 * LANES), 1)
    first = (lane2 % R_DK) < (R_DK // 2)
    cos = cos_ref[...]
    sin = sin_ref[...]

    def rope(x):
        xr = jnp.where(first, pltpu.roll(x, 2 * LANES - R_DK // 2, 1), pltpu.roll(x, R_DK // 2, 1))
        return x * cos + xr * sin

    ri = lax.broadcasted_iota(jnp.int32, (C, 1), 0)
    rif = ri.astype(F32)
    valid = ri < T
    di = lax.broadcasted_iota(jnp.int32, (C, C), 0)
    dj = lax.broadcasted_iota(jnp.int32, (C, C), 1)
    dd = jnp.maximum(di - dj, 0).astype(F32)
    causal = di >= dj
    lane = lax.broadcasted_iota(jnp.int32, (C, LANES), 1)
    lo = lane < R_DK
    row = lax.broadcasted_iota(jnp.int32, (LANES, LANES), 0)
    rlo = row < R_DK

    def one(g):
        zg, og, sg = src_ref.at[g], o_ref.at[g], st_ref.at[g]
        q = rope(zg[:, 0:256]) * (R_DK ** -0.5)
        k = rope(zg[:, 256:512])
        for p in range(R_HEADS // 2):
            qp = q[:, p * LANES:(p + 1) * LANES]
            kp = jnp.where(valid, k[:, p * LANES:(p + 1) * LANES], 0.0)
            sp = sg[p]
            ds = []
            for e in range(2):
                h = 2 * p + e
                lg = LOG_G[h]
                qm = jnp.where(lo if e == 0 else jnp.logical_not(lo), qp, 0.0)
                decay = jnp.where(causal, jnp.exp(lg * dd), 0.0)
                wq = jnp.exp(lg * (rif + 1.0))
                wk = jnp.exp(lg * (float(T - 1) - rif))
                vh = zg[:, 512 + h * LANES:512 + (h + 1) * LANES]
                s = _bdot_nt(qm, kp) * decay
                o = _bdot(s, vh) + _bdot(qm * wq, sp)
                mu = jnp.mean(o, axis=-1, keepdims=True)
                oc = o - mu
                var = jnp.mean(oc * oc, axis=-1, keepdims=True)
                gate = zg[:, 1024 + h * LANES:1024 + (h + 1) * LANES]
                og[:, h * LANES:(h + 1) * LANES] = (
                    oc * lax.rsqrt(var + EPS) * gn_ref[:, h * LANES:(h + 1) * LANES] * _silu(gate))[0:R].astype(og.dtype)
                ds.append(_bdot_tn(kp * wk, vh))
            gdec = jnp.where(rlo, math.exp(LOG_G[2 * p] * T), math.exp(LOG_G[2 * p + 1] * T))
            sg[p] = gdec * sp + jnp.where(rlo, ds[0], ds[1])

    for g in range(z_ref.shape[0]):
        one(g)

    @pl.when(c == pl.num_programs(1) - 1)
    def _():
        sl_ref[...] = st_ref[...]


def _retention(z, zcol, cos, sin, gn, s0, *, B, n, C, T):
    G = SEQS
    R = z.shape[1] // n
    kern = functools.partial(_ret_kernel, C=C, T=T)
    return pl.pallas_call(
        kern,
        grid=(B // G, n),
        in_specs=[
            pl.BlockSpec((G, R, 1536), lambda b, c: (b, c, zcol)),
            pl.BlockSpec((C, 256), lambda b, c: (c, 0)),
            pl.BlockSpec((C, 256), lambda b, c: (c, 0)),
            pl.BlockSpec((1, 512), lambda b, c: (0, 0)),
            pl.BlockSpec((G, 2, LANES, LANES), lambda b, c: (b, 0, 0, 0)),
        ],
        out_specs=[
            pl.BlockSpec((G, R, 512), lambda b, c: (b, c, 0)),
            pl.BlockSpec((G, 2, LANES, LANES), lambda b, c: (b, 0, 0, 0)),
        ],
        out_shape=[jax.ShapeDtypeStruct((B, n * R, 512), BF16 if R == C else F32),
                   jax.ShapeDtypeStruct((B, 2, LANES, LANES), F32)],
        scratch_shapes=[pltpu.VMEM((G, 2, LANES, LANES), F32)] + ([pltpu.VMEM((G, C, 1536), F32)] if R < C else []),
        compiler_params=_cparams(("arbitrary", "arbitrary")),
        name="retention",
    )(z, cos, sin, gn, s0)


def _mamba_kernel(z_ref, dt_ref, cw_ref, cb_ref, dtb_ref, alog_ref, dvec_ref, ng_ref, cbuf_ref, h0_ref,
                  o_ref, cnew_ref, hl_ref, xp_ref, st_ref, *, C, T):
    c = pl.program_id(1)
    K = M_CONV

    if z_ref.shape[1] < C:
        @pl.when((pl.program_id(0) == 0) & (c == 0))
        def _():
            xp_ref[...] = jnp.zeros_like(xp_ref)

    @pl.when(c == 0)
    def _():
        xp_ref[:, SUB - (K - 1):SUB, :] = cbuf_ref[...]
        st_ref[...] = h0_ref[...]

    lane = lax.broadcasted_iota(jnp.int32, (C, LANES), 1)
    ri = lax.broadcasted_iota(jnp.int32, (C, LANES), 0)
    di = lax.broadcasted_iota(jnp.int32, (C, C), 0)
    dj = lax.broadcasted_iota(jnp.int32, (C, C), 1)
    causal = di >= dj
    lo = lane < M_HD
    rlo = lax.broadcasted_iota(jnp.int32, (LANES, LANES), 0) < M_HD
    eye8 = _eye_rows(SUB, LANES)
    neg_a = -jnp.exp(alog_ref[...])

    for sq in range(z_ref.shape[0]):
        _mamba_one(z_ref.at[sq], dt_ref.at[sq], cw_ref, cb_ref, dtb_ref, dvec_ref, ng_ref, o_ref.at[sq],
                   xp_ref.at[sq], st_ref.at[sq], lane, ri, causal, lo, rlo, eye8, neg_a, C=C, T=T)

    @pl.when(c == pl.num_programs(1) - 1)
    def _():
        cnew_ref[...] = xp_ref[:, SUB - (K - 1):SUB, :]
        hl_ref[...] = st_ref[...]


def _mamba_one(z_ref, dt_ref, cw_ref, cb_ref, dtb_ref, dvec_ref, ng_ref, o_ref, xp_ref, st_ref,
               lane, ri, causal, lo, rlo, eye8, neg_a, *, C, T):
    K = M_CONV
    R = z_ref.shape[0]
    xp_ref[SUB:SUB + R, :] = z_ref[:, 0:M_CONV_DIM]
    conv = cb_ref[...]
    for j in range(K):
        conv = conv + xp_ref[SUB - (K - 1) + j:SUB - (K - 1) + j + C, :] * cw_ref[j:j + 1, :]
    tail = xp_ref[SUB + T - (K - 1):SUB + T, :]
    xp_ref[SUB - (K - 1):SUB, :] = tail
    xbc = _silu(conv)

    dt_raw = dt_ref[...]
    if R < C:
        dt_raw = jnp.concatenate([dt_raw, jnp.zeros((C - R, LANES), F32)], axis=0)
    dt = _softplus(dt_raw + dtb_ref[...])
    dt = jnp.where((lane < M_HEADS) & (ri < T), dt, 0.0)
    a = dt * neg_a
    a_cs = _cumsum_rows(a)
    a_row = _exact_rows(eye8, a_cs)
    dt_row = _exact_rows(eye8, dt)
    a_last = a_cs[C - 1:C, :]
    w_end = jnp.exp(a_last - a_cs) * dt
    ea = jnp.exp(a_cs)
    e_last = jnp.exp(a_last)

    ys = []
    heads_per_group = M_HEADS // M_GROUPS
    for g in range(M_GROUPS):
        bg = xbc[:, M_WIDTH + g * M_STATE:M_WIDTH + (g + 1) * M_STATE]
        cg = xbc[:, M_WIDTH + (M_GROUPS + g) * M_STATE:M_WIDTH + (M_GROUPS + g + 1) * M_STATE]
        cbm = _bdot_nt(cg, bg)
        for pp in range(heads_per_group // 2):
            p = g * (heads_per_group // 2) + pp
            xpair = xbc[:, p * LANES:(p + 1) * LANES]
            yh = []
            for e in range(2):
                h = 2 * p + e
                seg = a_cs[:, h:h + 1] - a_row[h:h + 1, :]
                lm = jnp.exp(jnp.where(causal, seg, NEG))
                m = cbm * lm * dt_row[h:h + 1, :]
                yh.append(_bdot(m, xpair))
            sp = st_ref[p]
            ycross = _bdot_nt(cg, sp) * jnp.where(lo, ea[:, 2 * p:2 * p + 1], ea[:, 2 * p + 1:2 * p + 2])
            y = jnp.where(lo, yh[0], yh[1]) + ycross + dvec_ref[:, p * LANES:(p + 1) * LANES] * xpair
            ys.append(y)
            xs = xpair * jnp.where(lo, w_end[:, 2 * p:2 * p + 1], w_end[:, 2 * p + 1:2 * p + 2])
            dstate = _bdot_tn(xs, bg)
            dec = jnp.where(rlo, e_last[:, 2 * p:2 * p + 1], e_last[:, 2 * p + 1:2 * p + 2])
            st_ref[p] = dec * sp + dstate

    my = jnp.concatenate(ys, axis=1)[0:R]
    gated = my * _silu(z_ref[:, M_CONV_DIM:M_CONV_DIM + M_WIDTH])
    o_ref[...] = (gated * lax.rsqrt(jnp.mean(gated * gated, axis=-1, keepdims=True) + EPS) * ng_ref[...]).astype(o_ref.dtype)


def _mamba(z, zcol, small, scol, cw, cb, dtb, alog, dvec, ng, cbuf, h0, *, B, n, C, T):
    G = SEQS
    R = z.shape[1] // n
    kern = functools.partial(_mamba_kernel, C=C, T=T)
    np_ = M_HEADS // 2
    return pl.pallas_call(
        kern,
        grid=(B // G, n),
        in_specs=[
            pl.BlockSpec((G, R, 1536), lambda b, c: (b, c, zcol)),
            pl.BlockSpec((G, R, LANES), lambda b, c: (b, c, scol)),
            pl.BlockSpec((M_CONV, M_CONV_DIM), lambda b, c: (0, 0)),
            pl.BlockSpec((1, M_CONV_DIM), lambda b, c: (0, 0)),
            pl.BlockSpec((1, LANES), lambda b, c: (0, 0)),
            pl.BlockSpec((1, LANES), lambda b, c: (0, 0)),
            pl.BlockSpec((1, M_WIDTH), lambda b, c: (0, 0)),
            pl.BlockSpec((1, M_WIDTH), lambda b, c: (0, 0)),
            pl.BlockSpec((G, M_CONV - 1, M_CONV_DIM), lambda b, c: (b, 0, 0)),
            pl.BlockSpec((G, np_, LANES, LANES), lambda b, c: (b, 0, 0, 0)),
        ],
        out_specs=[
            pl.BlockSpec((G, R, M_WIDTH), lambda b, c: (b, c, 0)),
            pl.BlockSpec((G, M_CONV - 1, M_CONV_DIM), lambda b, c: (b, 0, 0)),
            pl.BlockSpec((G, np_, LANES, LANES), lambda b, c: (b, 0, 0, 0)),
        ],
        out_shape=[
            jax.ShapeDtypeStruct((B, n * R, M_WIDTH), BF16 if R == C else F32),
            jax.ShapeDtypeStruct((B, M_CONV - 1, M_CONV_DIM), F32),
            jax.ShapeDtypeStruct((B, np_, LANES, LANES), F32),
        ],
        scratch_shapes=[pltpu.VMEM((G, SUB + C, M_CONV_DIM), F32), pltpu.VMEM((G, np_, LANES, LANES), F32)],
        compiler_params=_cparams(("arbitrary", "arbitrary")),
        name="mamba",
    )(z, small, cw, cb, dtb, alog, dvec, ng, cbuf, h0)


def _fox_qkv(z_ref, ff_ref, gq_ref, gk_ref, fb_ref, m64_ref):
    m64 = m64_ref[...]

    def headnorm(x, g):
        x2 = x * x
        hi = x2.astype(BF16)
        lo = (x2 - hi.astype(F32)).astype(BF16)
        ms = jnp.dot(hi, m64, preferred_element_type=F32) + jnp.dot(lo, m64, preferred_element_type=F32)
        return x * lax.rsqrt(ms + EPS) * g

    qs = headnorm(z_ref[:, 0:F_WIDTH], gq_ref[...]) * (F_DH ** -0.5)
    kn = headnorm(z_ref[:, F_WIDTH:2 * F_WIDTH], gk_ref[...])
    v = z_ref[:, 2 * F_WIDTH:3 * F_WIDTH]
    lane = lax.broadcasted_iota(jnp.int32, ff_ref.shape, 1)
    logf = jnp.where(lane < F_HEADS, _log_sigmoid(ff_ref[...] + fb_ref[...]), 0.0)
    return qs, kn, v, logf


def _foxprep_prompt_kernel(z_ref, ff_ref, gq_ref, gk_ref, fb_ref, m64_ref, exp_ref,
                           qt_ref, kb_ref, kft_ref, vft_ref, vt_ref, lft_ref, ccol_ref, crow_ref, carry_ref, *, tm):
    qs, kn, v, logf = _fox_qkv(z_ref, ff_ref, gq_ref, gk_ref, fb_ref, m64_ref)
    kb_ref[...] = kn.astype(BF16)
    eye8 = _eye_rows(SUB, LANES)
    lft_ref[...] = _exact_rows(eye8, logf)
    qs2 = qs * LOG2E
    for p in range(F_HEADS // 2):
        cols = slice(p * LANES, (p + 1) * LANES)
        qt_ref[p] = jnp.transpose(qs2[:, cols]).astype(BF16)
        kft_ref[cols, :] = jnp.transpose(kn[:, cols])
        vtp = jnp.transpose(v[:, cols])
        vft_ref[cols, :] = vtp
        vt_ref[p] = vtp.astype(BF16)

    @pl.when(pl.program_id(1) == 0)
    def _():
        carry_ref[...] = jnp.zeros_like(carry_ref)

    cs = _cumsum_rows(logf) + carry_ref[...]
    carry_ref[...] = cs[tm - 1:tm, :]
    cs2 = cs * LOG2E
    ccol_ref[...] = _exact_dot(cs2, exp_ref[...])
    crow = _exact_rows(eye8, cs2)
    for p in range(F_HEADS // 2):
        crow_ref[p] = crow[2 * p:2 * p + 2, :]


def _foxprep_prompt(z, small, gq, gk, fb, m64, expand, *, B, nb, tm):
    nt = B * nb * tm
    kern = functools.partial(_foxprep_prompt_kernel, tm=tm)
    zc = Z_FOX // 2048
    npair = F_HEADS // 2
    row = lambda b, i: (b * nb + i, 0)
    const = lambda b, i: (0, 0)
    return pl.pallas_call(
        kern,
        grid=(B, nb),
        in_specs=[
            pl.BlockSpec((tm, 2048), lambda b, i: (b * nb + i, zc)),
            pl.BlockSpec((tm, LANES), row),
            pl.BlockSpec((1, F_WIDTH), const),
            pl.BlockSpec((1, F_WIDTH), const),
            pl.BlockSpec((1, LANES), const),
            pl.BlockSpec((F_WIDTH, F_WIDTH), const),
            pl.BlockSpec((LANES, F_HEADS * LANES), const),
        ],
        out_specs=[
            pl.BlockSpec((None, npair, None, LANES, tm), lambda b, i: (b, 0, i, 0, 0)),
            pl.BlockSpec((tm, F_WIDTH), row),
            pl.BlockSpec((None, F_WIDTH, tm), lambda b, i: (b, 0, i)),
            pl.BlockSpec((None, F_WIDTH, tm), lambda b, i: (b, 0, i)),
            pl.BlockSpec((None, npair, None, LANES, tm), lambda b, i: (b, 0, i, 0, 0)),
            pl.BlockSpec((None, F_HEADS, tm), lambda b, i: (b, 0, i)),
            pl.BlockSpec((tm, F_HEADS * LANES), row),
            pl.BlockSpec((None, None, npair, 2, tm), lambda b, i: (b, i, 0, 0, 0)),
        ],
        out_shape=[
            jax.ShapeDtypeStruct((B, npair, nb, LANES, tm), BF16),
            jax.ShapeDtypeStruct((nt, F_WIDTH), BF16),
            jax.ShapeDtypeStruct((B, F_WIDTH, nb * tm), F32),
            jax.ShapeDtypeStruct((B, F_WIDTH, nb * tm), F32),
            jax.ShapeDtypeStruct((B, npair, nb, LANES, tm), BF16),
            jax.ShapeDtypeStruct((B, F_HEADS, nb * tm), F32),
            jax.ShapeDtypeStruct((nt, F_HEADS * LANES), F32),
            jax.ShapeDtypeStruct((B, nb, npair, 2, tm), F32),
        ],
        scratch_shapes=[pltpu.VMEM((1, LANES), F32)],
        compiler_params=_cparams(("arbitrary", "arbitrary")),
        name="foxprep_prompt",
    )(z, small, gq, gk, fb, m64, expand)


def _foxprep_sample_kernel(z_ref, ff_ref, gq_ref, gk_ref, fb_ref, m64_ref, qs_ref, kf_ref, vf_ref, lf_ref):
    qs, kn, v, logf = _fox_qkv(z_ref, ff_ref, gq_ref, gk_ref, fb_ref, m64_ref)
    qs_ref[...] = qs
    kf_ref[...] = kn
    vf_ref[...] = v
    lf_ref[...] = logf


def _foxprep_sample(z, small, gq, gk, fb, m64):
    nt = z.shape[0]
    zc = Z_FOX // 2048
    row = lambda i: (i, 0)
    const = lambda i: (0, 0)
    return pl.pallas_call(
        _foxprep_sample_kernel,
        grid=(1,),
        in_specs=[
            pl.BlockSpec((nt, 2048), lambda i: (i, zc)),
            pl.BlockSpec((nt, LANES), row),
            pl.BlockSpec((1, F_WIDTH), const),
            pl.BlockSpec((1, F_WIDTH), const),
            pl.BlockSpec((1, LANES), const),
            pl.BlockSpec((F_WIDTH, F_WIDTH), const),
        ],
        out_specs=[
            pl.BlockSpec((nt, F_WIDTH), row),
            pl.BlockSpec((nt, F_WIDTH), row),
            pl.BlockSpec((nt, F_WIDTH), row),
            pl.BlockSpec((nt, LANES), row),
        ],
        out_shape=[
            jax.ShapeDtypeStruct((nt, F_WIDTH), F32),
            jax.ShapeDtypeStruct((nt, F_WIDTH), F32),
            jax.ShapeDtypeStruct((nt, F_WIDTH), F32),
            jax.ShapeDtypeStruct((nt, LANES), F32),
        ],
        compiler_params=_cparams(("arbitrary",)),
        name="foxprep_sample",
    )(z, small, gq, gk, fb, m64)


def _foxattn_kernel(qt_ref, k_ref, vt_ref, cq_ref, cke_ref, cko_ref, fg_ref, o_ref, m_ref, l_ref, acc_ref, s_ref,
                    *, tq):
    qi = pl.program_id(2)
    qt = qt_ref[...]
    rlo = lax.broadcasted_iota(jnp.int32, (LANES, tq), 0) < F_DH
    zero = jnp.zeros_like(qt)
    q2t = jnp.concatenate([jnp.where(rlo, qt, zero), jnp.where(rlo, zero, qt)], axis=1)
    cq2 = jnp.concatenate([cq_ref[0:1, :], cq_ref[1:2, :]], axis=1)
    m_ref[...] = jnp.full_like(m_ref, NEG)
    l_ref[...] = jnp.zeros_like(l_ref)
    acc_ref[...] = jnp.zeros_like(acc_ref)
    kpos = lax.broadcasted_iota(jnp.int32, (tq, 2 * tq), 0)
    qpos = lax.broadcasted_iota(jnp.int32, (tq, 2 * tq), 1) % tq
    causal = kpos <= qpos

    strips = list(range(0, 2 * tq, STRIP))

    def keys(j):
        return k_ref[pl.ds(pl.multiple_of(j * tq, tq), tq), :]

    def qk(k, c0):
        return jnp.dot(k, q2t[:, c0:c0 + STRIP], preferred_element_type=F32)

    def block(j, masked):
        slot = j % 2
        start = pl.multiple_of(j * tq, tq)
        vt = vt_ref[j]
        cks = (cke_ref[pl.ds(start, tq), :], cko_ref[pl.ds(start, tq), :])
        k_next = None if masked else keys(j + 1)
        m_all, l_all, acc_all = m_ref[...], l_ref[...], acc_ref[...]
        m_out, l_out, acc_out = [], [], []
        for c0 in strips:
            cols = slice(c0, c0 + STRIP)
            st = s_ref[slot, :, cols]
            nxt = None if masked else qk(k_next, c0)
            u = st - jnp.concatenate([cks[c0 // tq]] * (STRIP // LANES), axis=1)
            if masked:
                u = jnp.where(causal[:, cols], u, NEG)
            cq = cq2[:, cols]
            m_prev = m_all[:, cols]
            m_new = jnp.maximum(m_prev, jnp.max(u, axis=0, keepdims=True) + cq)
            alpha = jnp.exp2(m_prev - m_new)
            p = jnp.exp2(u + (cq - m_new))
            l_out.append(alpha * l_all[:, cols] + jnp.sum(p, axis=0, keepdims=True))
            acc_out.append(alpha * acc_all[:, cols] + jnp.dot(vt, p.astype(BF16), preferred_element_type=F32))
            m_out.append(m_new)
            if not masked:
                s_ref[1 - slot, :, cols] = nxt
        m_ref[...] = jnp.concatenate(m_out, axis=1)
        l_ref[...] = jnp.concatenate(l_out, axis=1)
        acc_ref[...] = jnp.concatenate(acc_out, axis=1)

    k0 = keys(0)
    for c0 in strips:
        s_ref[0, :, c0:c0 + STRIP] = qk(k0, c0)

    def body(j, carry):
        block(j, False)
        return carry

    lax.fori_loop(0, qi, body, 0)
    block(qi, True)
    on = acc_ref[...] / l_ref[...]
    ot = jnp.concatenate([on[0:F_DH, 0:tq], on[F_DH:2 * F_DH, tq:2 * tq]], axis=0)
    o_ref[...] = (jnp.transpose(ot) * _silu(fg_ref[...])).astype(o_ref.dtype)


def _foxattn(qt, kb, vt, ccol, crow, z, *, B, L, tq):
    nq = L // tq
    npair = F_HEADS // 2
    fgc = (Z_FOX + 3 * F_WIDTH) // LANES
    kern = functools.partial(_foxattn_kernel, tq=tq)
    return pl.pallas_call(
        kern,
        grid=(B, npair, nq),
        in_specs=[
            pl.BlockSpec((None, None, None, LANES, tq), lambda b, p, i: (b, p, i, 0, 0)),
            pl.BlockSpec((L, LANES), lambda b, p, i: (b, p)),
            pl.BlockSpec((None, None, nq, LANES, tq), lambda b, p, i: (b, p, 0, 0, 0)),
            pl.BlockSpec((None, None, None, 2, tq), lambda b, p, i: (b, i, p, 0, 0)),
            pl.BlockSpec((L, LANES), lambda b, p, i: (b, 2 * p)),
            pl.BlockSpec((L, LANES), lambda b, p, i: (b, 2 * p + 1)),
            pl.BlockSpec((tq, LANES), lambda b, p, i: (b * nq + i, fgc + p)),
        ],
        out_specs=pl.BlockSpec((tq, LANES), lambda b, p, i: (b * nq + i, p)),
        out_shape=jax.ShapeDtypeStruct((B * L, F_WIDTH), BF16),
        scratch_shapes=[pltpu.VMEM((1, 2 * tq), F32), pltpu.VMEM((1, 2 * tq), F32), pltpu.VMEM((LANES, 2 * tq), F32),
                        pltpu.VMEM((2, tq, 2 * tq), F32)],
        compiler_params=_cparams(("arbitrary", "arbitrary", "arbitrary")),
        name="foxattn",
    )(qt, kb, vt, crow, ccol, ccol, z)


def _pagecum_kernel(pt_ref, lf_hbm, ck_ref, tot_ref, g_ref, sem, *, n_pages, page_off):
    b = pl.program_id(0)

    def copy(bb, pg, slot):
        return pltpu.make_async_copy(lf_hbm.at[pt_ref[bb, pg] + page_off], g_ref.at[slot, pg], sem.at[slot])

    def start_all(bb, slot):
        def body(pg, c):
            copy(bb, pg, slot).start()
            return c
        lax.fori_loop(0, n_pages, body, 0)

    @pl.when(b == 0)
    def _():
        start_all(b, 0)

    @pl.when(b + 1 < pl.num_programs(0))
    def _():
        start_all(b + 1, (b + 1) % 2)

    slot = b % 2

    def wait(pg, c):
        copy(b, pg, slot).wait()
        return c

    lax.fori_loop(0, n_pages, wait, 0)

    rows = n_pages * F_HEADS
    w = g_ref[slot].reshape(rows, PAGE)
    row = lax.broadcasted_iota(jnp.int32, (rows, PAGE), 0)
    tri = (lax.broadcasted_iota(jnp.int32, (PAGE, PAGE), 0) <= lax.broadcasted_iota(jnp.int32, (PAGE, PAGE), 1)).astype(BF16)
    ones = jnp.ones((PAGE, PAGE), BF16)
    parts = _split3(w)
    w = sum(jnp.dot(h, tri, preferred_element_type=F32) for h in parts)
    tot = sum(jnp.dot(h, ones, preferred_element_type=F32) for h in parts)
    inc = tot
    s = F_HEADS
    while s < rows:
        inc = inc + jnp.where(row >= s, pltpu.roll(inc, s, 0), 0.0)
        s *= 2
    ck_ref[...] = (w + (inc - tot)).reshape(n_pages, F_HEADS, PAGE)
    last = inc[rows - F_HEADS:rows, :]
    eye = (lax.broadcasted_iota(jnp.int32, (F_HEADS, LANES), 0) == lax.broadcasted_iota(jnp.int32, (F_HEADS, LANES), 1))
    tot_ref[...] = jnp.sum(jnp.where(eye, last, 0.0), axis=0, keepdims=True)


def _pagecum(page_table, lf_cache, *, page_off):
    bd, n_pages = page_table.shape
    kern = functools.partial(_pagecum_kernel, n_pages=n_pages, page_off=page_off)
    return pl.pallas_call(
        kern,
        grid_spec=pltpu.PrefetchScalarGridSpec(
            num_scalar_prefetch=1,
            grid=(bd,),
            in_specs=[pl.BlockSpec(memory_space=pl.ANY)],
            out_specs=[
                pl.BlockSpec((None, n_pages, F_HEADS, PAGE), lambda b, pt: (b, 0, 0, 0)),
                pl.BlockSpec((None, 1, LANES), lambda b, pt: (b, 0, 0)),
            ],
            scratch_shapes=[pltpu.VMEM((2, n_pages, F_HEADS, PAGE), F32), pltpu.SemaphoreType.DMA((2,))],
        ),
        out_shape=[
            jax.ShapeDtypeStruct((bd, n_pages, F_HEADS, PAGE), F32),
            jax.ShapeDtypeStruct((bd, 1, LANES), F32),
        ],
        compiler_params=_cparams(("arbitrary",)),
        name="pagecum",
    )(page_table, lf_cache)


def _decode_kernel(pt_ref, q_ref, kn_ref, vn_ref, lfn_ref, tot_ref, ck_ref, fg_ref, k_hbm, v_hbm,
                   o_ref, kbuf, vbuf, sem, qbd_ref, cq_ref, cn_ref, m_ref, l_ref, acc_ref,
                   *, T, ppb, ns, page_off):
    b = pl.program_id(0)
    s = pl.program_id(1)
    t = b * ns + s
    nt = pl.num_programs(0) * ns
    R = T * F_HEADS

    def copies(bb, ss, slot):
        out = []
        for i in range(ppb):
            page = pt_ref[bb, ss * ppb + i] + page_off
            out.append(pltpu.make_async_copy(k_hbm.at[page], kbuf.at[slot, i], sem.at[0, slot]))
            out.append(pltpu.make_async_copy(v_hbm.at[page], vbuf.at[slot, i], sem.at[1, slot]))
        return out

    @pl.when(t == 0)
    def _():
        for cp in copies(b, s, 0):
            cp.start()

    @pl.when(t + 1 < nt)
    def _():
        t1 = t + 1
        for cp in copies(t1 // ns, t1 % ns, t1 % 2):
            cp.start()

    rl = lax.broadcasted_iota(jnp.int32, (SUB, F_WIDTH), 1) // F_DH
    rh = lax.broadcasted_iota(jnp.int32, (SUB, F_WIDTH), 0)
    hmask = rl == rh

    @pl.when(s == 0)
    def _():
        q = q_ref[...].astype(F32)
        qbd_ref[...] = jnp.concatenate(
            [jnp.where(hmask, jnp.broadcast_to(q[tt:tt + 1, :], (SUB, F_WIDTH)), 0.0) for tt in range(T)],
            axis=0).astype(BF16)
        cn = tot_ref[...] + _cumsum_rows(lfn_ref[...])
        cn_ref[...] = cn
        eye = (lax.broadcasted_iota(jnp.int32, (SUB, LANES), 0) == lax.broadcasted_iota(jnp.int32, (SUB, LANES), 1))
        cq_ref[...] = jnp.concatenate(
            [jnp.sum(jnp.where(eye, jnp.broadcast_to(cn[tt:tt + 1, :], (SUB, LANES)), 0.0), axis=-1, keepdims=True)
             for tt in range(T)], axis=0)
        m_ref[...] = jnp.full_like(m_ref, NEG)
        l_ref[...] = jnp.zeros_like(l_ref)
        acc_ref[...] = jnp.zeros_like(acc_ref)

    slot = t % 2
    for cp in copies(b, s, slot):
        cp.wait()

    def update(sc, v):
        m_prev = m_ref[...]
        m_new = jnp.maximum(m_prev, jnp.max(sc, axis=-1, keepdims=True))
        alpha = jnp.exp(m_prev - m_new)
        p = jnp.exp(sc - m_new)
        l_ref[...] = alpha * l_ref[...] + jnp.sum(p, axis=-1, keepdims=True)
        acc_ref[...] = alpha * acc_ref[...] + jnp.dot(p.astype(BF16), v, preferred_element_type=F32)
        m_ref[...] = m_new

    qbd = qbd_ref[...]
    cq = cq_ref[...]
    sc = jnp.concatenate(
        [jnp.dot(qbd, kbuf[slot, i].astype(BF16), preferred_element_type=F32) for i in range(ppb)], axis=1)
    ck = jnp.concatenate([ck_ref[i] for i in range(ppb)], axis=1)
    sc = sc + cq - jnp.concatenate([ck] * T, axis=0)
    m_prev = m_ref[...]
    m_new = jnp.maximum(m_prev, jnp.max(sc, axis=-1, keepdims=True))
    alpha = jnp.exp(m_prev - m_new)
    pf = jnp.exp(sc - m_new)
    l_ref[...] = alpha * l_ref[...] + jnp.sum(pf, axis=-1, keepdims=True)
    p = pf.astype(BF16)
    pv =lax.dot_general(p[:, 0:PAGE], vbuf[slot, 0].astype(BF16), NT_DIMS, preferred_element_type=F32)
    for i in range(1, ppb):
        pv = pv + lax.dot_general(p[:, i * PAGE:(i + 1) * PAGE], vbuf[slot, i].astype(BF16), NT_DIMS,
                                  preferred_element_type=F32)
    acc_ref[...] = alpha * acc_ref[...] + pv
    m_ref[...] = m_new

    @pl.when(s == ns - 1)
    def _():
        cn = cn_ref[...]
        cn_pad = jnp.concatenate([cn, jnp.zeros((PAGE - SUB, LANES), F32)], axis=0)
        cn_row = _exact_rows(_eye_rows(SUB, LANES), cn_pad)
        zpad = jnp.zeros((PAGE - TPAD, F_WIDTH), F32)
        kn = jnp.concatenate([kn_ref[...], zpad], axis=0).astype(BF16)
        vn = jnp.concatenate([vn_ref[...], zpad], axis=0).astype(BF16)
        sn = lax.dot_general(qbd, kn, NT_DIMS, preferred_element_type=F32)
        sn = sn + cq - jnp.concatenate([cn_row] * T, axis=0)
        rt = lax.broadcasted_iota(jnp.int32, (R, PAGE), 0) // F_HEADS
        ct = lax.broadcasted_iota(jnp.int32, (R, PAGE), 1)
        sn = jnp.where((ct <= rt) & (ct < T), sn, NEG)
        update(sn, vn)
        on = acc_ref[...] / l_ref[...]
        for tt in range(T):
            blk = jnp.where(hmask, on[tt * F_HEADS:(tt + 1) * F_HEADS, :], 0.0)
            o_ref[tt:tt + 1, :] = jnp.sum(blk, axis=0, keepdims=True) * _silu(fg_ref[tt:tt + 1, :])
        o_ref[T:SUB, :] = jnp.zeros((SUB - T, F_WIDTH), F32)


def _decode(page_table, qb, kn_pad, vn_pad, lf, tot, ck, z, k_cache, v_cache, *, T, page_off, ppb):
    bd, n_pages = page_table.shape
    ns = n_pages // ppb
    R = T * F_HEADS
    fgc = (Z_FOX + 3 * F_WIDTH) // F_WIDTH
    kern = functools.partial(_decode_kernel, T=T, ppb=ppb, ns=ns, page_off=page_off)
    return pl.pallas_call(
        kern,
        grid_spec=pltpu.PrefetchScalarGridSpec(
            num_scalar_prefetch=1,
            grid=(bd, ns),
            in_specs=[
                pl.BlockSpec((TPAD, F_WIDTH), lambda b, s, pt: (b, 0)),
                pl.BlockSpec((TPAD, F_WIDTH), lambda b, s, pt: (b, 0)),
                pl.BlockSpec((TPAD, F_WIDTH), lambda b, s, pt: (b, 0)),
                pl.BlockSpec((TPAD, LANES), lambda b, s, pt: (b, 0)),
                pl.BlockSpec((None, 1, LANES), lambda b, s, pt: (b, 0, 0)),
                pl.BlockSpec((None, ppb, F_HEADS, PAGE), lambda b, s, pt: (b, s, 0, 0)),
                pl.BlockSpec((TPAD, F_WIDTH), lambda b, s, pt: (b, fgc)),
                pl.BlockSpec(memory_space=pl.ANY),
                pl.BlockSpec(memory_space=pl.ANY),
            ],
            out_specs=pl.BlockSpec((TPAD, F_WIDTH), lambda b, s, pt: (b, 0)),
            scratch_shapes=[
                pltpu.VMEM((2, ppb, F_WIDTH, PAGE), F32),
                pltpu.VMEM((2, ppb, F_WIDTH, PAGE), F32),
                pltpu.SemaphoreType.DMA((2, 2)),
                pltpu.VMEM((R, F_WIDTH), BF16),
                pltpu.VMEM((R, 1), F32),
                pltpu.VMEM((SUB, LANES), F32),
                pltpu.VMEM((R, 1), F32),
                pltpu.VMEM((R, 1), F32),
                pltpu.VMEM((R, F_WIDTH), F32),
            ],
        ),
        out_shape=jax.ShapeDtypeStruct((bd * TPAD, F_WIDTH), F32),
        compiler_params=_cparams(("arbitrary", "arbitrary")),
        name="decode",
    )(page_table, qb, kn_pad, vn_pad, lf, tot, ck, z, k_cache, v_cache)


def _merge_kernel(x_ref, r_ref, f_ref, m_ref, gate_ref, wr_ref, wf_ref, wm_ref, wo_ref, y_ref):
    br = _bdot(r_ref[...], wr_ref[...])
    bf = _bdot(f_ref[...], wf_ref[...])
    bm = _bdot(m_ref[...], wm_ref[...])
    merged = (jax.nn.sigmoid(gate_ref[:, 0:D_MODEL]) * br
              + jax.nn.sigmoid(gate_ref[:, D_MODEL:2 * D_MODEL]) * bf
              + jax.nn.sigmoid(gate_ref[:, 2 * D_MODEL:3 * D_MODEL]) * bm)
    y_ref[...] = x_ref[...] + _bdot(merged, wo_ref[...])


def _merge(x, r, f, m, z, wr, wf, wm, wo):
    nt = x.shape[0]
    tm = min(nt, 512)
    row = lambda i: (i, 0)
    full = lambda i: (0, 0)
    return pl.pallas_call(
        _merge_kernel,
        grid=(nt // tm,),
        in_specs=[
            pl.BlockSpec((tm, D_MODEL), row),
            pl.BlockSpec((tm, 512), row),
            pl.BlockSpec((tm, 512), row),
            pl.BlockSpec((tm, 512), row),
            pl.BlockSpec((tm, 3 * D_MODEL), lambda i: (i, Z_GATE // (3 * D_MODEL))),
            pl.BlockSpec((512, D_MODEL), full),
            pl.BlockSpec((512, D_MODEL), full),
            pl.BlockSpec((512, D_MODEL), full),
            pl.BlockSpec((D_MODEL, D_MODEL), full),
        ],
        out_specs=pl.BlockSpec((tm, D_MODEL), row),
        out_shape=jax.ShapeDtypeStruct((nt, D_MODEL), F32),
        compiler_params=_cparams(("arbitrary",)),
        name="merge",
    )(x, r, f, m, z, wr, wf, wm, wo)


def _rope_tables(pos):
    inv = ROPE_BASE ** (-jnp.arange(0, R_DK, 2, dtype=F32) / R_DK)
    ang = pos.astype(F32)[:, None] * inv[None, :]
    cos, sin = jnp.cos(ang), jnp.sin(ang)
    cos_h = jnp.concatenate([cos, cos], axis=1)
    sin_h = jnp.concatenate([-sin, sin], axis=1)
    return jnp.tile(cos_h, (1, R_HEADS)), jnp.tile(sin_h, (1, R_HEADS))


def _pad_lanes(v, off=0, width=LANES):
    return jnp.zeros((1, width), F32).at[0, off:off + v.shape[0]].set(v)


def _layer_weights(l, norm_g, w_in, r_norm_g, f_qnorm_g, f_knorm_g, f_bias, m_conv_w, m_conv_b,
                   m_dt_bias, m_A_log, m_D, m_norm_g, w_br_r, w_br_f, w_br_m, w_out):
    w = w_in[l]
    w_main = jnp.concatenate(
        [w[:, 0:1536], w[:, 3592:4616], w[:, 4624:5136], w[:, 5136:8208], w[:, 1536:3072], w[:, 3080:3592]],
        axis=1).astype(BF16)
    zpad = jnp.zeros((D_MODEL, LANES - F_HEADS), F32)
    w_small = jnp.concatenate([w[:, 3072:3080], zpad, w[:, 4616:4624], zpad], axis=1).astype(BF16)
    return dict(
        g=norm_g[l][None, :], w_main=w_main, w_small=w_small,
        gn=r_norm_g[l][None, :],
        gq=jnp.tile(f_qnorm_g[l], F_HEADS)[None, :], gk=jnp.tile(f_knorm_g[l], F_HEADS)[None, :],
        fb=_pad_lanes(f_bias[l]),
        cw=m_conv_w[l], cb=m_conv_b[l][None, :],
        dtb=_pad_lanes(m_dt_bias[l]), alog=_pad_lanes(m_A_log[l]),
        dvec=jnp.repeat(m_D[l], M_HD)[None, :], ng=m_norm_g[l][None, :],
        wr=w_br_r[l].astype(BF16), wf=w_br_f[l].astype(BF16), wm=w_br_m[l].astype(BF16), wo=w_out[l].astype(BF16),
    )


def _consts():
    i = np.arange(F_WIDTH)
    m64 = jnp.asarray((i[:, None] // F_DH == i[None, :] // F_DH).astype(np.float32) / F_DH, dtype=BF16)
    j = np.arange(F_HEADS * LANES)
    expand = jnp.asarray((np.arange(LANES)[:, None] == j[None, :] // LANES).astype(np.float32), dtype=BF16)
    return m64, expand


def kernel(x_prompt, x_sample, cache_fox_k, cache_fox_v, cache_fox_logf, page_table, state_ret, state_conv, state_ssm, norm_g, w_in, r_norm_g, f_qnorm_g, f_knorm_g, f_bias, m_conv_w, m_conv_b, m_dt_bias, m_A_log, m_D, m_norm_g, w_br_r, w_br_f, w_br_m, w_out):
    B, L, D = x_prompt.shape
    Bd, T, _ = x_sample.shape
    depth = w_in.shape[0]
    n_pool = cache_fox_k.shape[1]
    n_pages = page_table.shape[1]
    past = n_pages * PAGE
    C = CHUNK
    n = L // C
    tq = min(L, 512)
    ppb = min(n_pages, 16)

    m64, expand = _consts()
    cos_p, sin_p = _rope_tables(jnp.arange(L))
    cos_s, sin_s = _rope_tables(past + jnp.arange(C))
    kc = jnp.transpose(cache_fox_k, (0, 1, 3, 4, 2)).reshape(depth * n_pool, F_WIDTH, PAGE)
    vc = jnp.transpose(cache_fox_v, (0, 1, 3, 4, 2)).reshape(depth * n_pool, F_WIDTH, PAGE)
    lc = jnp.transpose(cache_fox_logf, (0, 1, 3, 2)).reshape(depth * n_pool, F_HEADS, PAGE)

    hp = x_prompt.reshape(B * L, D)
    hs = jnp.pad(x_sample, ((0, 0), (0, TPAD - T), (0, 0))).reshape(Bd * TPAD, D)
    zeros_ret = jnp.zeros((B, 2, LANES, LANES), F32)
    zeros_conv = jnp.zeros((B, M_CONV - 1, M_CONV_DIM), F32)
    zeros_ssm = jnp.zeros((B, M_HEADS // 2, LANES, LANES), F32)

    outs_p, outs_s = [], []
    for l in range(depth):
        W = _layer_weights(l, norm_g, w_in, r_norm_g, f_qnorm_g, f_knorm_g, f_bias, m_conv_w, m_conv_b,
                           m_dt_bias, m_A_log, m_D, m_norm_g, w_br_r, w_br_f, w_br_m, w_out)
        z, small = _inproj(hp, W["g"], W["w_main"], W["w_small"])
        z3, small3 = z.reshape(B, L, Z_WIDTH), small.reshape(B, L, SMALL_W)
        r_out, ret_p = _retention(z3, Z_RET // 1536, cos_p, sin_p, W["gn"], zeros_ret, B=B, n=n, C=C, T=C)
        qt, kb, kft, vft, vt, lft, ccol, crow = _foxprep_prompt(
            z, small, W["gq"], W["gk"], W["fb"], m64, expand, B=B, nb=L // tq, tm=tq)
        f_out = _foxattn(qt, kb, vt, ccol, crow, z, B=B, L=L, tq=tq)
        m_out, conv_p, ssm_p = _mamba(z3, Z_MAM // 1536, small3, 1, W["cw"], W["cb"], W["dtb"], W["alog"],
                                      W["dvec"], W["ng"], zeros_conv, zeros_ssm, B=B, n=n, C=C, T=C)
        hp = _merge(hp, r_out.reshape(B * L, 512), f_out, m_out.reshape(B * L, M_WIDTH), z,
                    W["wr"], W["wf"], W["wm"], W["wo"])
        outs_p.append((kft, vft, lft, ret_p, conv_p, ssm_p))

        zs, smalls = _inproj(hs, W["g"], W["w_main"], W["w_small"])
        zs3, smalls3 = zs.reshape(Bd, TPAD, Z_WIDTH), smalls.reshape(Bd, TPAD, SMALL_W)
        r_s, ret_s = _retention(zs3, Z_RET // 1536, cos_s, sin_s, W["gn"],
                                state_ret[l].reshape(Bd, 2, LANES, LANES), B=Bd, n=1, C=C, T=T)
        qs, kf, vf, lf = _foxprep_sample(zs, smalls, W["gq"], W["gk"], W["fb"], m64)
        ck, tot = _pagecum(page_table, lc, page_off=l * n_pool)
        f_out = _decode(page_table, qs, kf, vf, lf, tot, ck, zs, kc, vc, T=T, page_off=l * n_pool, ppb=ppb)
        m_s, conv_s, ssm_s = _mamba(zs3, Z_MAM // 1536, smalls3, 1,
                                    W["cw"], W["cb"], W["dtb"], W["alog"], W["dvec"], W["ng"],
                                    state_conv[l], state_ssm[l].reshape(Bd, M_HEADS // 2, LANES, LANES),
                                    B=Bd, n=1, C=C, T=T)
        hs = _merge(hs, r_s.reshape(Bd * TPAD, 512), f_out, m_s.reshape(Bd * TPAD, M_WIDTH), zs,
                    W["wr"], W["wf"], W["wm"], W["wo"])
        sel = lambda a: a.reshape(Bd, TPAD, -1)[:, :T]
        outs_s.append((sel(kf), sel(vf), sel(lf)[:, :, :F_HEADS], ret_s, conv_s, ssm_s))

    stk = lambda outs, i: jnp.stack([o[i] for o in outs])
    return (
        hp.reshape(B, L, D),
        hs.reshape(Bd, TPAD, D)[:, :T],
        jnp.transpose(stk(outs_p, 0).reshape(depth, B, F_HEADS, F_DH, L), (0, 1, 4, 2, 3)),
        jnp.transpose(stk(outs_p, 1).reshape(depth, B, F_HEADS, F_DH, L), (0, 1, 4, 2, 3)),
        jnp.transpose(stk(outs_p, 2), (0, 1, 3, 2)),
        stk(outs_p, 3).reshape(depth, B, R_HEADS, R_DK, R_DV),
        stk(outs_p, 4),
        stk(outs_p, 5).reshape(depth, B, M_HEADS, M_HD, M_STATE),
        stk(outs_s, 0).reshape(depth, Bd, T, F_HEADS, F_DH),
        stk(outs_s, 1).reshape(depth, Bd, T, F_HEADS, F_DH),
        stk(outs_s, 2),
        stk(outs_s, 3).reshape(depth, Bd, R_HEADS, R_DK, R_DV),
        stk(outs_s, 4),
        stk(outs_s, 5).reshape(depth, Bd, M_HEADS, M_HD, M_STATE),
    )
```

```python
import functools
import math

import numpy as np
import jax
import jax.numpy as jnp
from jax import lax
from jax.experimental import pallas as pl
from jax.experimental.pallas import tpu as pltpu

F32 = jnp.float32
BF16 = jnp.bfloat16

D_MODEL = 1024
PAGE = 128
R_HEADS, R_DK, R_DV = 4, 64, 128
F_HEADS, F_DH = 8, 64
F_WIDTH = F_HEADS * F_DH
M_HEADS, M_HD, M_GROUPS, M_STATE, M_CONV = 8, 64, 2, 128, 4
M_WIDTH = M_HEADS * M_HD
M_CONV_DIM = M_WIDTH + 2 * M_GROUPS * M_STATE
CHUNK = 128
ROPE_BASE = 10000.0
EPS = 1e-6
NEG = -1e30
LOG2E = 1.4426950408889634

Z_RET, Z_MAM, Z_GATE, Z_FOX = 0, 1536, 3072, 6144
Z_WIDTH = 8192
SMALL_W = 256
LANES = 128
SUB = 8
TPAD = 8
STRIP = 256
SEQS = 4

VMEM_LIMIT = 56 * 1024 * 1024

NT_DIMS = (((1,), (1,)), ((), ()))
TN_DIMS = (((0,), (0,)), ((), ()))

LOG_G = [float(np.log1p(-np.exp2(-5.0 - h))) for h in range(R_HEADS)]


def _cparams(sem):
    return pltpu.CompilerParams(dimension_semantics=sem, vmem_limit_bytes=VMEM_LIMIT)


def _silu(x):
    return x * jax.nn.sigmoid(x)


def _softplus(x):
    return jnp.maximum(x, 0.0) + jnp.log1p(jnp.exp(-jnp.abs(x)))


def _log_sigmoid(x):
    return jnp.minimum(x, 0.0) - jnp.log1p(jnp.exp(-jnp.abs(x)))


def _bdot(a, b):
    return jnp.dot(a.astype(BF16), b.astype(BF16), preferred_element_type=F32)


def _bdot_nt(a, b):
    return lax.dot_general(a.astype(BF16), b.astype(BF16), NT_DIMS, preferred_element_type=F32)


def _bdot_tn(a, b):
    return lax.dot_general(a.astype(BF16), b.astype(BF16), TN_DIMS, preferred_element_type=F32)


def _split3(x):
    h1 = x.astype(BF16)
    r1 = x - h1.astype(F32)
    h2 = r1.astype(BF16)
    h3 = (r1 - h2.astype(F32)).astype(BF16)
    return h1, h2, h3


def _exact_dot(x, sel):
    h1, h2, h3 = _split3(x)
    d = lambda a: jnp.dot(a, sel, preferred_element_type=F32)
    return d(h1) + d(h2) + d(h3)


def _exact_rows(sel, x):
    h1, h2, h3 = _split3(x)
    d = lambda a: lax.dot_general(sel, a, NT_DIMS, preferred_element_type=F32)
    return d(h1) + d(h2) + d(h3)


def _cumsum_rows(x):
    n = x.shape[0]
    i = lax.broadcasted_iota(jnp.int32, x.shape, 0)
    s = 1
    while s < n:
        x = x + jnp.where(i >= s, pltpu.roll(x, s, 0), 0.0)
        s *= 2
    return x


def _eye_rows(r, k):
    return (lax.broadcasted_iota(jnp.int32, (r, k), 0) == lax.broadcasted_iota(jnp.int32, (r, k), 1)).astype(BF16)


def _inproj_kernel(x_ref, g_ref, w_ref, ws_ref, z_ref, s_ref, h_ref):
    @pl.when(pl.program_id(1) == 0)
    def _():
        x = x_ref[...]
        r = x * lax.rsqrt(jnp.mean(x * x, axis=-1, keepdims=True) + EPS)
        h = (r * g_ref[...]).astype(BF16)
        h_ref[...] = h
        s_ref[...] = jnp.dot(h, ws_ref[...], preferred_element_type=F32)

    z_ref[...] = jnp.dot(h_ref[...], w_ref[...], preferred_element_type=F32)


def _inproj(x, g, w_main, w_small):
    nt = x.shape[0]
    tm = next((t for t in (2048, 1024, 512, 256) if nt % t == 0), nt)
    tn = 1024
    return pl.pallas_call(
        _inproj_kernel,
        grid=(nt // tm, Z_WIDTH // tn),
        in_specs=[
            pl.BlockSpec((tm, D_MODEL), lambda i, j: (i, 0)),
            pl.BlockSpec((1, D_MODEL), lambda i, j: (0, 0)),
            pl.BlockSpec((D_MODEL, tn), lambda i, j: (0, j)),
            pl.BlockSpec((D_MODEL, SMALL_W), lambda i, j: (0, 0)),
        ],
        out_specs=[
            pl.BlockSpec((tm, tn), lambda i, j: (i, j)),
            pl.BlockSpec((tm, SMALL_W), lambda i, j: (i, 0)),
        ],
        out_shape=[jax.ShapeDtypeStruct((nt, Z_WIDTH), F32), jax.ShapeDtypeStruct((nt, SMALL_W), F32)],
        scratch_shapes=[pltpu.VMEM((tm, D_MODEL), BF16)],
        compiler_params=_cparams(("arbitrary", "arbitrary")),
        name="inproj",
    )(x, g, w_main, w_small)


def _ret_kernel(z_ref, cos_ref, sin_ref, gn_ref, s0_ref, o_ref, sl_ref, st_ref, *zp, C, T):
    c = pl.program_id(1)
    R = z_ref.shape[1]

    @pl.when(c == 0)
    def _():
        st_ref[...] = s0_ref[...]

    if R < C:
        src_ref, = zp

        @pl.when((pl.program_id(0) == 0) & (c == 0))
        def _():
            src_ref[...] = jnp.zeros_like(src_ref)

        src_ref[:, 0:R, :] = z_ref[...]
    else:
        src_ref = z_ref

    lane2 = lax.broadcasted_iota(jnp.int32, (C, 2 * LANES), 1)
    first = (lane2 % R_DK) < (R_DK // 2)
    cos = cos_ref[...]
    sin = sin_ref[...]

    def rope(x):
        xr = jnp.where(first, pltpu.roll(x, 2 * LANES - R_DK // 2, 1), pltpu.roll(x, R_DK // 2, 1))
        return x * cos + xr * sin

    ri = lax.broadcasted_iota(jnp.int32, (C, 1), 0)
    rif = ri.astype(F32)
    valid = ri < T
    di = lax.broadcasted_iota(jnp.int32, (C, C), 0)
    dj = lax.broadcasted_iota(jnp.int32, (C, C), 1)
    dd = jnp.maximum(di - dj, 0).astype(F32)
    causal = di >= dj
    lane = lax.broadcasted_iota(jnp.int32, (C, LANES), 1)
    lo = lane < R_DK
    row = lax.broadcasted_iota(jnp.int32, (LANES, LANES), 0)
    rlo = row < R_DK

    def one(g):
        zg, og, sg = src_ref.at[g], o_ref.at[g], st_ref.at[g]
        q = rope(zg[:, 0:256]) * (R_DK ** -0.5)
        k = rope(zg[:, 256:512])
        for p in range(R_HEADS // 2):
            qp = q[:, p * LANES:(p + 1) * LANES]
            kp = jnp.where(valid, k[:, p * LANES:(p + 1) * LANES], 0.0)
            sp = sg[p]
            ds = []
            for e in range(2):
                h = 2 * p + e
                lg = LOG_G[h]
                qm = jnp.where(lo if e == 0 else jnp.logical_not(lo), qp, 0.0)
                decay = jnp.where(causal, jnp.exp(lg * dd), 0.0)
                wq = jnp.exp(lg * (rif + 1.0))
                wk = jnp.exp(lg * (float(T - 1) - rif))
                vh = zg[:, 512 + h * LANES:512 + (h + 1) * LANES]
                s = _bdot_nt(qm, kp) * decay
                o = _bdot(s, vh) + _bdot(qm * wq, sp)
                mu = jnp.mean(o, axis=-1, keepdims=True)
                oc = o - mu
                var = jnp.mean(oc * oc, axis=-1, keepdims=True)
                gate = zg[:, 1024 + h * LANES:1024 + (h + 1) * LANES]
                og[:, h * LANES:(h + 1) * LANES] = (
                    oc * lax.rsqrt(var + EPS) * gn_ref[:, h * LANES:(h + 1) * LANES] * _silu(gate))[0:R].astype(og.dtype)
                ds.append(_bdot_tn(kp * wk, vh))
            gdec = jnp.where(rlo, math.exp(LOG_G[2 * p] * T), math.exp(LOG_G[2 * p + 1] * T))
            sg[p] = gdec * sp + jnp.where(rlo, ds[0], ds[1])

    for g in range(z_ref.shape[0]):
        one(g)

    @pl.when(c == pl.num_programs(1) - 1)
    def _():
        sl_ref[...] = st_ref[...]


def _retention(z, zcol, cos, sin, gn, s0, *, B, n, C, T):
    G = SEQS
    R = z.shape[1] // n
    kern = functools.partial(_ret_kernel, C=C, T=T)
    return pl.pallas_call(
        kern,
        grid=(B // G, n),
        in_specs=[
            pl.BlockSpec((G, R, 1536), lambda b, c: (b, c, zcol)),
            pl.BlockSpec((C, 256), lambda b, c: (c, 0)),
            pl.BlockSpec((C, 256), lambda b, c: (c, 0)),
            pl.BlockSpec((1, 512), lambda b, c: (0, 0)),
            pl.BlockSpec((G, 2, LANES, LANES), lambda b, c: (b, 0, 0, 0)),
        ],
        out_specs=[
            pl.BlockSpec((G, R, 512), lambda b, c: (b, c, 0)),
            pl.BlockSpec((G, 2, LANES, LANES), lambda b, c: (b, 0, 0, 0)),
        ],
        out_shape=[jax.ShapeDtypeStruct((B, n * R, 512), BF16 if R == C else F32),
                   jax.ShapeDtypeStruct((B, 2, LANES, LANES), F32)],
        scratch_shapes=[pltpu.VMEM((G, 2, LANES, LANES), F32)] + ([pltpu.VMEM((G, C, 1536), F32)] if R < C else []),
        compiler_params=_cparams(("arbitrary", "arbitrary")),
        name="retention",
    )(z, cos, sin, gn, s0)


def _mamba_kernel(z_ref, dt_ref, cw_ref, cb_ref, dtb_ref, alog_ref, dvec_ref, ng_ref, cbuf_ref, h0_ref,
                  o_ref, cnew_ref, hl_ref, xp_ref, st_ref, *, C, T):
    c = pl.program_id(1)
    K = M_CONV

    if z_ref.shape[1] < C:
        @pl.when((pl.program_id(0) == 0) & (c == 0))
        def _():
            xp_ref[...] = jnp.zeros_like(xp_ref)

    @pl.when(c == 0)
    def _():
        xp_ref[:, SUB - (K - 1):SUB, :] = cbuf_ref[...]
        st_ref[...] = h0_ref[...]

    lane = lax.broadcasted_iota(jnp.int32, (C, LANES), 1)
    ri = lax.broadcasted_iota(jnp.int32, (C, LANES), 0)
    di = lax.broadcasted_iota(jnp.int32, (C, C), 0)
    dj = lax.broadcasted_iota(jnp.int32, (C, C), 1)
    causal = di >= dj
    lo = lane < M_HD
    rlo = lax.broadcasted_iota(jnp.int32, (LANES, LANES), 0) < M_HD
    eye8 = _eye_rows(SUB, LANES)
    neg_a = -jnp.exp(alog_ref[...])

    for sq in range(z_ref.shape[0]):
        _mamba_one(z_ref.at[sq], dt_ref.at[sq], cw_ref, cb_ref, dtb_ref, dvec_ref, ng_ref, o_ref.at[sq],
                   xp_ref.at[sq], st_ref.at[sq], lane, ri, causal, lo, rlo, eye8, neg_a, C=C, T=T)

    @pl.when(c == pl.num_programs(1) - 1)
    def _():
        cnew_ref[...] = xp_ref[:, SUB - (K - 1):SUB, :]
        hl_ref[...] = st_ref[...]


def _mamba_one(z_ref, dt_ref, cw_ref, cb_ref, dtb_ref, dvec_ref, ng_ref, o_ref, xp_ref, st_ref,
               lane, ri, causal, lo, rlo, eye8, neg_a, *, C, T):
    K = M_CONV
    R = z_ref.shape[0]
    xp_ref[SUB:SUB + R, :] = z_ref[:, 0:M_CONV_DIM]
    conv = cb_ref[...]
    for j in range(K):
        conv = conv + xp_ref[SUB - (K - 1) + j:SUB - (K - 1) + j + C, :] * cw_ref[j:j + 1, :]
    tail = xp_ref[SUB + T - (K - 1):SUB + T, :]
    xp_ref[SUB - (K - 1):SUB, :] = tail
    xbc = _silu(conv)

    dt_raw = dt_ref[...]
    if R < C:
        dt_raw = jnp.concatenate([dt_raw, jnp.zeros((C - R, LANES), F32)], axis=0)
    dt = _softplus(dt_raw + dtb_ref[...])
    dt = jnp.where((lane < M_HEADS) & (ri < T), dt, 0.0)
    a = dt * neg_a
    a_cs = _cumsum_rows(a)
    a_row = _exact_rows(eye8, a_cs)
    dt_row = _exact_rows(eye8, dt)
    a_last = a_cs[C - 1:C, :]
    w_end = jnp.exp(a_last - a_cs) * dt
    ea = jnp.exp(a_cs)
    e_last = jnp.exp(a_last)

    ys = []
    heads_per_group = M_HEADS // M_GROUPS
    for g in range(M_GROUPS):
        bg = xbc[:, M_WIDTH + g * M_STATE:M_WIDTH + (g + 1) * M_STATE]
        cg = xbc[:, M_WIDTH + (M_GROUPS + g) * M_STATE:M_WIDTH + (M_GROUPS + g + 1) * M_STATE]
        cbm = _bdot_nt(cg, bg)
        for pp in range(heads_per_group // 2):
            p = g * (heads_per_group // 2) + pp
            xpair = xbc[:, p * LANES:(p + 1) * LANES]
            yh = []
            for e in range(2):
                h = 2 * p + e
                seg = a_cs[:, h:h + 1] - a_row[h:h + 1, :]
                lm = jnp.exp(jnp.where(causal, seg, NEG))
                m = cbm * lm * dt_row[h:h + 1, :]
                yh.append(_bdot(m, xpair))
            sp = st_ref[p]
            ycross = _bdot_nt(cg, sp) * jnp.where(lo, ea[:, 2 * p:2 * p + 1], ea[:, 2 * p + 1:2 * p + 2])
            y = jnp.where(lo, yh[0], yh[1]) + ycross + dvec_ref[:, p * LANES:(p + 1) * LANES] * xpair
            ys.append(y)
            xs = xpair * jnp.where(lo, w_end[:, 2 * p:2 * p + 1], w_end[:, 2 * p + 1:2 * p + 2])
            dstate = _bdot_tn(xs, bg)
            dec = jnp.where(rlo, e_last[:, 2 * p:2 * p + 1], e_last[:, 2 * p + 1:2 * p + 2])
            st_ref[p] = dec * sp + dstate

    my = jnp.concatenate(ys, axis=1)[0:R]
    gated = my * _silu(z_ref[:, M_CONV_DIM:M_CONV_DIM + M_WIDTH])
    o_ref[...] = (gated * lax.rsqrt(jnp.mean(gated * gated, axis=-1, keepdims=True) + EPS) * ng_ref[...]).astype(o_ref.dtype)


def _mamba(z, zcol, small, scol, cw, cb, dtb, alog, dvec, ng, cbuf, h0, *, B, n, C, T):
    G = SEQS
    R = z.shape[1] // n
    kern = functools.partial(_mamba_kernel, C=C, T=T)
    np_ = M_HEADS // 2
    return pl.pallas_call(
        kern,
        grid=(B // G, n),
        in_specs=[
            pl.BlockSpec((G, R, 1536), lambda b, c: (b, c, zcol)),
            pl.BlockSpec((G, R, LANES), lambda b, c: (b, c, scol)),
            pl.BlockSpec((M_CONV, M_CONV_DIM), lambda b, c: (0, 0)),
            pl.BlockSpec((1, M_CONV_DIM), lambda b, c: (0, 0)),
            pl.BlockSpec((1, LANES), lambda b, c: (0, 0)),
            pl.BlockSpec((1, LANES), lambda b, c: (0, 0)),
            pl.BlockSpec((1, M_WIDTH), lambda b, c: (0, 0)),
            pl.BlockSpec((1, M_WIDTH), lambda b, c: (0, 0)),
            pl.BlockSpec((G, M_CONV - 1, M_CONV_DIM), lambda b, c: (b, 0, 0)),
            pl.BlockSpec((G, np_, LANES, LANES), lambda b, c: (b, 0, 0, 0)),
        ],
        out_specs=[
            pl.BlockSpec((G, R, M_WIDTH), lambda b, c: (b, c, 0)),
            pl.BlockSpec((G, M_CONV - 1, M_CONV_DIM), lambda b, c: (b, 0, 0)),
            pl.BlockSpec((G, np_, LANES, LANES), lambda b, c: (b, 0, 0, 0)),
        ],
        out_shape=[
            jax.ShapeDtypeStruct((B, n * R, M_WIDTH), BF16 if R == C else F32),
            jax.ShapeDtypeStruct((B, M_CONV - 1, M_CONV_DIM), F32),
            jax.ShapeDtypeStruct((B, np_, LANES, LANES), F32),
        ],
        scratch_shapes=[pltpu.VMEM((G, SUB + C, M_CONV_DIM), F32), pltpu.VMEM((G, np_, LANES, LANES), F32)],
        compiler_params=_cparams(("arbitrary", "arbitrary")),
        name="mamba",
    )(z, small, cw, cb, dtb, alog, dvec, ng, cbuf, h0)


def _fox_qkv(z_ref, ff_ref, gq_ref, gk_ref, fb_ref, m64_ref):
    m64 = m64_ref[...]

    def headnorm(x, g):
        x2 = x * x
        hi = x2.astype(BF16)
        lo = (x2 - hi.astype(F32)).astype(BF16)
        ms = jnp.dot(hi, m64, preferred_element_type=F32) + jnp.dot(lo, m64, preferred_element_type=F32)
        return x * lax.rsqrt(ms + EPS) * g

    qs = headnorm(z_ref[:, 0:F_WIDTH], gq_ref[...]) * (F_DH ** -0.5)
    kn = headnorm(z_ref[:, F_WIDTH:2 * F_WIDTH], gk_ref[...])
    v = z_ref[:, 2 * F_WIDTH:3 * F_WIDTH]
    lane = lax.broadcasted_iota(jnp.int32, ff_ref.shape, 1)
    logf = jnp.where(lane < F_HEADS, _log_sigmoid(ff_ref[...] + fb_ref[...]), 0.0)
    return qs, kn, v, logf


def _foxprep_prompt_kernel(z_ref, ff_ref, gq_ref, gk_ref, fb_ref, m64_ref, exp_ref,
                           qt_ref, kb_ref, kft_ref, vft_ref, vt_ref, lft_ref, ccol_ref, crow_ref, carry_ref, *, tm):
    qs, kn, v, logf = _fox_qkv(z_ref, ff_ref, gq_ref, gk_ref, fb_ref, m64_ref)
    kb_ref[...] = kn.astype(BF16)
    eye8 = _eye_rows(SUB, LANES)
    lft_ref[...] = _exact_rows(eye8, logf)
    qs2 = qs * LOG2E
    for p in range(F_HEADS // 2):
        cols = slice(p * LANES, (p + 1) * LANES)
        qt_ref[p] = jnp.transpose(qs2[:, cols]).astype(BF16)
        kft_ref[cols, :] = jnp.transpose(kn[:, cols])
        vtp = jnp.transpose(v[:, cols])
        vft_ref[cols, :] = vtp
        vt_ref[p] = vtp.astype(BF16)

    @pl.when(pl.program_id(1) == 0)
    def _():
        carry_ref[...] = jnp.zeros_like(carry_ref)

    cs = _cumsum_rows(logf) + carry_ref[...]
    carry_ref[...] = cs[tm - 1:tm, :]
    cs2 = cs * LOG2E
    ccol_ref[...] = _exact_dot(cs2, exp_ref[...])
    crow = _exact_rows(eye8, cs2)
    for p in range(F_HEADS // 2):
        crow_ref[p] = crow[2 * p:2 * p + 2, :]


def _foxprep_prompt(z, small, gq, gk, fb, m64, expand, *, B, nb, tm):
    nt = B * nb * tm
    kern = functools.partial(_foxprep_prompt_kernel, tm=tm)
    zc = Z_FOX // 2048
    npair = F_HEADS // 2
    row = lambda b, i: (b * nb + i, 0)
    const = lambda b, i: (0, 0)
    return pl.pallas_call(
        kern,
        grid=(B, nb),
        in_specs=[
            pl.BlockSpec((tm, 2048), lambda b, i: (b * nb + i, zc)),
            pl.BlockSpec((tm, LANES), row),
            pl.BlockSpec((1, F_WIDTH), const),
            pl.BlockSpec((1, F_WIDTH), const),
            pl.BlockSpec((1, LANES), const),
            pl.BlockSpec((F_WIDTH, F_WIDTH), const),
            pl.BlockSpec((LANES, F_HEADS * LANES), const),
        ],
        out_specs=[
            pl.BlockSpec((None, npair, None, LANES, tm), lambda b, i: (b, 0, i, 0, 0)),
            pl.BlockSpec((tm, F_WIDTH), row),
            pl.BlockSpec((None, F_WIDTH, tm), lambda b, i: (b, 0, i)),
            pl.BlockSpec((None, F_WIDTH, tm), lambda b, i: (b, 0, i)),
            pl.BlockSpec((None, npair, None, LANES, tm), lambda b, i: (b, 0, i, 0, 0)),
            pl.BlockSpec((None, F_HEADS, tm), lambda b, i: (b, 0, i)),
            pl.BlockSpec((tm, F_HEADS * LANES), row),
            pl.BlockSpec((None, None, npair, 2, tm), lambda b, i: (b, i, 0, 0, 0)),
        ],
        out_shape=[
            jax.ShapeDtypeStruct((B, npair, nb, LANES, tm), BF16),
            jax.ShapeDtypeStruct((nt, F_WIDTH), BF16),
            jax.ShapeDtypeStruct((B, F_WIDTH, nb * tm), F32),
            jax.ShapeDtypeStruct((B, F_WIDTH, nb * tm), F32),
            jax.ShapeDtypeStruct((B, npair, nb, LANES, tm), BF16),
            jax.ShapeDtypeStruct((B, F_HEADS, nb * tm), F32),
            jax.ShapeDtypeStruct((nt, F_HEADS * LANES), F32),
            jax.ShapeDtypeStruct((B, nb, npair, 2, tm), F32),
        ],
        scratch_shapes=[pltpu.VMEM((1, LANES), F32)],
        compiler_params=_cparams(("arbitrary", "arbitrary")),
        name="foxprep_prompt",
    )(z, small, gq, gk, fb, m64, expand)


def _foxprep_sample_kernel(z_ref, ff_ref, gq_ref, gk_ref, fb_ref, m64_ref, qs_ref, kf_ref, vf_ref, lf_ref):
    qs, kn, v, logf = _fox_qkv(z_ref, ff_ref, gq_ref, gk_ref, fb_ref, m64_ref)
    qs_ref[...] = qs
    kf_ref[...] = kn
    vf_ref[...] = v
    lf_ref[...] = logf


def _foxprep_sample(z, small, gq, gk, fb, m64):
    nt = z.shape[0]
    zc = Z_FOX // 2048
    row = lambda i: (i, 0)
    const = lambda i: (0, 0)
    return pl.pallas_call(
        _foxprep_sample_kernel,
        grid=(1,),
        in_specs=[
            pl.BlockSpec((nt, 2048), lambda i: (i, zc)),
            pl.BlockSpec((nt, LANES), row),
            pl.BlockSpec((1, F_WIDTH), const),
            pl.BlockSpec((1, F_WIDTH), const),
            pl.BlockSpec((1, LANES), const),
            pl.BlockSpec((F_WIDTH, F_WIDTH), const),
        ],
        out_specs=[
            pl.BlockSpec((nt, F_WIDTH), row),
            pl.BlockSpec((nt, F_WIDTH), row),
            pl.BlockSpec((nt, F_WIDTH), row),
            pl.BlockSpec((nt, LANES), row),
        ],
        out_shape=[
            jax.ShapeDtypeStruct((nt, F_WIDTH), F32),
            jax.ShapeDtypeStruct((nt, F_WIDTH), F32),
            jax.ShapeDtypeStruct((nt, F_WIDTH), F32),
            jax.ShapeDtypeStruct((nt, LANES), F32),
        ],
        compiler_params=_cparams(("arbitrary",)),
        name="foxprep_sample",
    )(z, small, gq, gk, fb, m64)


def _foxattn_kernel(qt_ref, k_ref, vt_ref, cq_ref, cke_ref, cko_ref, fg_ref, o_ref, m_ref, l_ref, acc_ref, s_ref,
                    *, tq):
    qi = pl.program_id(2)
    nq = pl.num_programs(2)
    rlo = lax.broadcasted_iota(jnp.int32, (LANES, tq), 0) < F_DH

    def two_heads(qt):
        zero = jnp.zeros_like(qt)
        return jnp.concatenate([jnp.where(rlo, qt, zero), jnp.where(rlo, zero, qt)], axis=1)

    q2t = two_heads(qt_ref[qi])
    q2t_next = two_heads(qt_ref[jnp.minimum(qi + 1, nq - 1)])
    cq2 = jnp.concatenate([cq_ref[0:1, :], cq_ref[1:2, :]], axis=1)
    m_ref[...] = jnp.full_like(m_ref, NEG)
    l_ref[...] = jnp.zeros_like(l_ref)
    acc_ref[...] = jnp.zeros_like(acc_ref)
    kpos = lax.broadcasted_iota(jnp.int32, (tq, 2 * tq), 0)
    qpos = lax.broadcasted_iota(jnp.int32, (tq, 2 * tq), 1) % tq
    causal = kpos <= qpos

    strips = list(range(0, 2 * tq, STRIP))

    def keys(j):
        return k_ref[pl.ds(pl.multiple_of(j * tq, tq), tq), :]

    def qk(k, q, c0):
        return jnp.dot(k, q[:, c0:c0 + STRIP], preferred_element_type=F32)

    def block(j, masked):
        slot = j % 2
        nslot = 0 if masked else 1 - slot
        start = pl.multiple_of(j * tq, tq)
        vt = vt_ref[j]
        cks = (cke_ref[pl.ds(start, tq), :], cko_ref[pl.ds(start, tq), :])
        k_next = keys(0) if masked else keys(j + 1)
        q_next = q2t_next if masked else q2t
        m_all, l_all, acc_all = m_ref[...], l_ref[...], acc_ref[...]
        m_out, l_out, acc_out = [], [], []
        for c0 in strips:
            cols = slice(c0, c0 + STRIP)
            st = s_ref[slot, :, cols]
            nxt = qk(k_next, q_next, c0)
            u = st - jnp.concatenate([cks[c0 // tq]] * (STRIP // LANES), axis=1)
            if masked:
                u = jnp.where(causal[:, cols], u, NEG)
            cq = cq2[:, cols]
            m_prev = m_all[:, cols]
            m_new = jnp.maximum(m_prev, jnp.max(u, axis=0, keepdims=True) + cq)
            alpha = jnp.exp2(m_prev - m_new)
            p = jnp.exp2(u + (cq - m_new))
            l_out.append(alpha * l_all[:, cols] + jnp.sum(p, axis=0, keepdims=True))
            acc_out.append(alpha * acc_all[:, cols] + jnp.dot(vt, p.astype(BF16), preferred_element_type=F32))
            m_out.append(m_new)
            s_ref[nslot, :, cols] = nxt
        m_ref[...] = jnp.concatenate(m_out, axis=1)
        l_ref[...] = jnp.concatenate(l_out, axis=1)
        acc_ref[...] = jnp.concatenate(acc_out, axis=1)

    @pl.when(qi == 0)
    def _():
        k0 = keys(0)
        for c0 in strips:
            s_ref[0, :, c0:c0 + STRIP] = qk(k0, q2t, c0)

    def body(j, carry):
        block(j, False)
        return carry

    lax.fori_loop(0, qi, body, 0)
    block(qi, True)
    on = acc_ref[...] / l_ref[...]
    ot = jnp.concatenate([on[0:F_DH, 0:tq], on[F_DH:2 * F_DH, tq:2 * tq]], axis=0)
    o_ref[...] = (jnp.transpose(ot) * _silu(fg_ref[...])).astype(o_ref.dtype)


def _foxattn(qt, kb, vt, ccol, crow, z, *, B, L, tq):
    nq = L // tq
    npair = F_HEADS // 2
    fgc = (Z_FOX + 3 * F_WIDTH) // LANES
    kern = functools.partial(_foxattn_kernel, tq=tq)
    return pl.pallas_call(
        kern,
        grid=(B, npair, nq),
        in_specs=[
            pl.BlockSpec((None, None, nq, LANES, tq), lambda b, p, i: (b, p, 0, 0, 0)),
            pl.BlockSpec((L, LANES), lambda b, p, i: (b, p)),
            pl.BlockSpec((None, None, nq, LANES, tq), lambda b, p, i: (b, p, 0, 0, 0)),
            pl.BlockSpec((None, None, None, 2, tq), lambda b, p, i: (b, i, p, 0, 0)),
            pl.BlockSpec((L, LANES), lambda b, p, i: (b, 2 * p)),
            pl.BlockSpec((L, LANES), lambda b, p, i: (b, 2 * p + 1)),
            pl.BlockSpec((tq, LANES), lambda b, p, i: (b * nq + i, fgc + p)),
        ],
        out_specs=pl.BlockSpec((tq, LANES), lambda b, p, i: (b * nq + i, p)),
        out_shape=jax.ShapeDtypeStruct((B * L, F_WIDTH), BF16),
        scratch_shapes=[pltpu.VMEM((1, 2 * tq), F32), pltpu.VMEM((1, 2 * tq), F32), pltpu.VMEM((LANES, 2 * tq), F32),
                        pltpu.VMEM((2, tq, 2 * tq), F32)],
        compiler_params=_cparams(("arbitrary", "arbitrary", "arbitrary")),
        name="foxattn",
    )(qt, kb, vt, crow, ccol, ccol, z)


def _pagecum_kernel(pt_ref, lf_hbm, ck_ref, tot_ref, g_ref, sem, *, n_pages, page_off):
    b = pl.program_id(0)

    def copy(bb, pg, slot):
        return pltpu.make_async_copy(lf_hbm.at[pt_ref[bb, pg] + page_off], g_ref.at[slot, pg], sem.at[slot])

    def start_all(bb, slot):
        def body(pg, c):
            copy(bb, pg, slot).start()
            return c
        lax.fori_loop(0, n_pages, body, 0)

    @pl.when(b == 0)
    def _():
        start_all(b, 0)

    @pl.when(b + 1 < pl.num_programs(0))
    def _():
        start_all(b + 1, (b + 1) % 2)

    slot = b % 2

    def wait(pg, c):
        copy(b, pg, slot).wait()
        return c

    lax.fori_loop(0, n_pages, wait, 0)

    rows = n_pages * F_HEADS
    w = g_ref[slot].reshape(rows, PAGE)
    row = lax.broadcasted_iota(jnp.int32, (rows, PAGE), 0)
    tri = (lax.broadcasted_iota(jnp.int32, (PAGE, PAGE), 0) <= lax.broadcasted_iota(jnp.int32, (PAGE, PAGE), 1)).astype(BF16)
    ones = jnp.ones((PAGE, PAGE), BF16)
    parts = _split3(w)
    w = sum(jnp.dot(h, tri, preferred_element_type=F32) for h in parts)
    tot = sum(jnp.dot(h, ones, preferred_element_type=F32) for h in parts)
    inc = tot
    s = F_HEADS
    while s < rows:
        inc = inc + jnp.where(row >= s, pltpu.roll(inc, s, 0), 0.0)
        s *= 2
    ck_ref[...] = (w + (inc - tot)).reshape(n_pages, F_HEADS, PAGE)
    last = inc[rows - F_HEADS:rows, :]
    eye = (lax.broadcasted_iota(jnp.int32, (F_HEADS, LANES), 0) == lax.broadcasted_iota(jnp.int32, (F_HEADS, LANES), 1))
    tot_ref[...] = jnp.sum(jnp.where(eye, last, 0.0), axis=0, keepdims=True)


def _pagecum(page_table, lf_cache, *, page_off):
    bd, n_pages = page_table.shape
    kern = functools.partial(_pagecum_kernel, n_pages=n_pages, page_off=page_off)
    return pl.pallas_call(
        kern,
        grid_spec=pltpu.PrefetchScalarGridSpec(
            num_scalar_prefetch=1,
            grid=(bd,),
            in_specs=[pl.BlockSpec(memory_space=pl.ANY)],
            out_specs=[
                pl.BlockSpec((None, n_pages, F_HEADS, PAGE), lambda b, pt: (b, 0, 0, 0)),
                pl.BlockSpec((None, 1, LANES), lambda b, pt: (b, 0, 0)),
            ],
            scratch_shapes=[pltpu.VMEM((2, n_pages, F_HEADS, PAGE), F32), pltpu.SemaphoreType.DMA((2,))],
        ),
        out_shape=[
            jax.ShapeDtypeStruct((bd, n_pages, F_HEADS, PAGE), F32),
            jax.ShapeDtypeStruct((bd, 1, LANES), F32),
        ],
        compiler_params=_cparams(("arbitrary",)),
        name="pagecum",
    )(page_table, lf_cache)


def _decode_kernel(pt_ref, q_ref, kn_ref, vn_ref, lfn_ref, tot_ref, ck_ref, fg_ref, k_hbm, v_hbm,
                   o_ref, kbuf, vbuf, sem, qbd_ref, cq_ref, cn_ref, m_ref, l_ref, acc_ref,
                   *, T, ppb, ns, page_off):
    b = pl.program_id(0)
    s = pl.program_id(1)
    t = b * ns + s
    nt = pl.num_programs(0) * ns
    R = T * F_HEADS

    def copies(bb, ss, slot):
        out = []
        for i in range(ppb):
            page = pt_ref[bb, ss * ppb + i] + page_off
            out.append(pltpu.make_async_copy(k_hbm.at[page], kbuf.at[slot, i], sem.at[0, slot]))
            out.append(pltpu.make_async_copy(v_hbm.at[page], vbuf.at[slot, i], sem.at[1, slot]))
        return out

    @pl.when(t == 0)
    def _():
        for cp in copies(b, s, 0):
            cp.start()

    @pl.when(t + 1 < nt)
    def _():
        t1 = t + 1
        for cp in copies(t1 // ns, t1 % ns, t1 % 2):
            cp.start()

    rl = lax.broadcasted_iota(jnp.int32, (SUB, F_WIDTH), 1) // F_DH
    rh = lax.broadcasted_iota(jnp.int32, (SUB, F_WIDTH), 0)
    hmask = rl == rh

    @pl.when(s == 0)
    def _():
        q = q_ref[...].astype(F32)
        qbd_ref[...] = jnp.concatenate(
            [jnp.where(hmask, jnp.broadcast_to(q[tt:tt + 1, :], (SUB, F_WIDTH)), 0.0) for tt in range(T)],
            axis=0).astype(BF16)
        cn = tot_ref[...] + _cumsum_rows(lfn_ref[...])
        cn_ref[...] = cn
        eye = (lax.broadcasted_iota(jnp.int32, (SUB, LANES), 0) == lax.broadcasted_iota(jnp.int32, (SUB, LANES), 1))
        cq_ref[...] = jnp.concatenate(
            [jnp.sum(jnp.where(eye, jnp.broadcast_to(cn[tt:tt + 1, :], (SUB, LANES)), 0.0), axis=-1, keepdims=True)
             for tt in range(T)], axis=0)
        m_ref[...] = jnp.full_like(m_ref, NEG)
        l_ref[...] = jnp.zeros_like(l_ref)
        acc_ref[...] = jnp.zeros_like(acc_ref)

    slot = t % 2
    for cp in copies(b, s, slot):
        cp.wait()

    def update(sc, v):
        m_prev = m_ref[...]
        m_new = jnp.maximum(m_prev, jnp.max(sc, axis=-1, keepdims=True))
        alpha = jnp.exp(m_prev - m_new)
        p = jnp.exp(sc - m_new)
        l_ref[...] = alpha * l_ref[...] + jnp.sum(p, axis=-1, keepdims=True)
        acc_ref[...] = alpha * acc_ref[...] + jnp.dot(p.astype(BF16), v, preferred_element_type=F32)
        m_ref[...] = m_new

    qbd = qbd_ref[...]
    cq = cq_ref[...]
    sc = jnp.concatenate(
        [jnp.dot(qbd, kbuf[slot, i].astype(BF16), preferred_element_type=F32) for i in range(ppb)], axis=1)
    ck = jnp.concatenate([ck_ref[i] for i in range(ppb)], axis=1)
    sc = sc + cq - jnp.concatenate([ck] * T, axis=0)
    m_prev = m_ref[...]
    m_new = jnp.maximum(m_prev, jnp.max(sc, axis=-1, keepdims=True))
    alpha = jnp.exp(m_prev - m_new)
    pf = jnp.exp(sc - m_new)
    l_ref[...] = alpha * l_ref[...] + jnp.sum(pf, axis=-1, keepdims=True)
    p = pf.astype(BF16)
    pv =lax.dot_general(p[:, 0:PAGE], vbuf[slot, 0].astype(BF16), NT_DIMS, preferred_element_type=F32)
    for i in range(1, ppb):
        pv = pv + lax.dot_general(p[:, i * PAGE:(i + 1) * PAGE], vbuf[slot, i].astype(BF16), NT_DIMS,
                                  preferred_element_type=F32)
    acc_ref[...] = alpha * acc_ref[...] + pv
    m_ref[...] = m_new

    @pl.when(s == ns - 1)
    def _():
        cn = cn_ref[...]
        cn_pad = jnp.concatenate([cn, jnp.zeros((PAGE - SUB, LANES), F32)], axis=0)
        cn_row = _exact_rows(_eye_rows(SUB, LANES), cn_pad)
        zpad = jnp.zeros((PAGE - TPAD, F_WIDTH), F32)
        kn = jnp.concatenate([kn_ref[...], zpad], axis=0).astype(BF16)
        vn = jnp.concatenate([vn_ref[...], zpad], axis=0).astype(BF16)
        sn = lax.dot_general(qbd, kn, NT_DIMS, preferred_element_type=F32)
        sn = sn + cq - jnp.concatenate([cn_row] * T, axis=0)
        rt = lax.broadcasted_iota(jnp.int32, (R, PAGE), 0) // F_HEADS
        ct = lax.broadcasted_iota(jnp.int32, (R, PAGE), 1)
        sn = jnp.where((ct <= rt) & (ct < T), sn, NEG)
        update(sn, vn)
        on = acc_ref[...] / l_ref[...]
        for tt in range(T):
            blk = jnp.where(hmask, on[tt * F_HEADS:(tt + 1) * F_HEADS, :], 0.0)
            o_ref[tt:tt + 1, :] = jnp.sum(blk, axis=0, keepdims=True) * _silu(fg_ref[tt:tt + 1, :])
        o_ref[T:SUB, :] = jnp.zeros((SUB - T, F_WIDTH), F32)


def _decode(page_table, qb, kn_pad, vn_pad, lf, tot, ck, z, k_cache, v_cache, *, T, page_off, ppb):
    bd, n_pages = page_table.shape
    ns = n_pages // ppb
    R = T * F_HEADS
    fgc = (Z_FOX + 3 * F_WIDTH) // F_WIDTH
    kern = functools.partial(_decode_kernel, T=T, ppb=ppb, ns=ns, page_off=page_off)
    return pl.pallas_call(
        kern,
        grid_spec=pltpu.PrefetchScalarGridSpec(
            num_scalar_prefetch=1,
            grid=(bd, ns),
            in_specs=[
                pl.BlockSpec((TPAD, F_WIDTH), lambda b, s, pt: (b, 0)),
                pl.BlockSpec((TPAD, F_WIDTH), lambda b, s, pt: (b, 0)),
                pl.BlockSpec((TPAD, F_WIDTH), lambda b, s, pt: (b, 0)),
                pl.BlockSpec((TPAD, LANES), lambda b, s, pt: (b, 0)),
                pl.BlockSpec((None, 1, LANES), lambda b, s, pt: (b, 0, 0)),
                pl.BlockSpec((None, ppb, F_HEADS, PAGE), lambda b, s, pt: (b, s, 0, 0)),
                pl.BlockSpec((TPAD, F_WIDTH), lambda b, s, pt: (b, fgc)),
                pl.BlockSpec(memory_space=pl.ANY),
                pl.BlockSpec(memory_space=pl.ANY),
            ],
            out_specs=pl.BlockSpec((TPAD, F_WIDTH), lambda b, s, pt: (b, 0)),
            scratch_shapes=[
                pltpu.VMEM((2, ppb, F_WIDTH, PAGE), F32),
                pltpu.VMEM((2, ppb, F_WIDTH, PAGE), F32),
                pltpu.SemaphoreType.DMA((2, 2)),
                pltpu.VMEM((R, F_WIDTH), BF16),
                pltpu.VMEM((R, 1), F32),
                pltpu.VMEM((SUB, LANES), F32),
                pltpu.VMEM((R, 1), F32),
                pltpu.VMEM((R, 1), F32),
                pltpu.VMEM((R, F_WIDTH), F32),
            ],
        ),
        out_shape=jax.ShapeDtypeStruct((bd * TPAD, F_WIDTH), F32),
        compiler_params=_cparams(("arbitrary", "arbitrary")),
        name="decode",
    )(page_table, qb, kn_pad, vn_pad, lf, tot, ck, z, k_cache, v_cache)


def _merge_kernel(x_ref, r_ref, f_ref, m_ref, gate_ref, wr_ref, wf_ref, wm_ref, wo_ref, y_ref):
    br = _bdot(r_ref[...], wr_ref[...])
    bf = _bdot(f_ref[...], wf_ref[...])
    bm = _bdot(m_ref[...], wm_ref[...])
    merged = (jax.nn.sigmoid(gate_ref[:, 0:D_MODEL]) * br
              + jax.nn.sigmoid(gate_ref[:, D_MODEL:2 * D_MODEL]) * bf
              + jax.nn.sigmoid(gate_ref[:, 2 * D_MODEL:3 * D_MODEL]) * bm)
    y_ref[...] = x_ref[...] + _bdot(merged, wo_ref[...])


def _merge(x, r, f, m, z, wr, wf, wm, wo):
    nt = x.shape[0]
    tm = min(nt, 512)
    row = lambda i: (i, 0)
    full = lambda i: (0, 0)
    return pl.pallas_call(
        _merge_kernel,
        grid=(nt // tm,),
        in_specs=[
            pl.BlockSpec((tm, D_MODEL), row),
            pl.BlockSpec((tm, 512), row),
            pl.BlockSpec((tm, 512), row),
            pl.BlockSpec((tm, 512), row),
            pl.BlockSpec((tm, 3 * D_MODEL), lambda i: (i, Z_GATE // (3 * D_MODEL))),
            pl.BlockSpec((512, D_MODEL), full),
            pl.BlockSpec((512, D_MODEL), full),
            pl.BlockSpec((512, D_MODEL), full),
            pl.BlockSpec((D_MODEL, D_MODEL), full),
        ],
        out_specs=pl.BlockSpec((tm, D_MODEL), row),
        out_shape=jax.ShapeDtypeStruct((nt, D_MODEL), F32),
        compiler_params=_cparams(("arbitrary",)),
        name="merge",
    )(x, r, f, m, z, wr, wf, wm, wo)


def _rope_tables(pos):
    inv = ROPE_BASE ** (-jnp.arange(0, R_DK, 2, dtype=F32) / R_DK)
    ang = pos.astype(F32)[:, None] * inv[None, :]
    cos, sin = jnp.cos(ang), jnp.sin(ang)
    cos_h = jnp.concatenate([cos, cos], axis=1)
    sin_h = jnp.concatenate([-sin, sin], axis=1)
    return jnp.tile(cos_h, (1, R_HEADS)), jnp.tile(sin_h, (1, R_HEADS))


def _pad_lanes(v, off=0, width=LANES):
    return jnp.zeros((1, width), F32).at[0, off:off + v.shape[0]].set(v)


def _layer_weights(l, norm_g, w_in, r_norm_g, f_qnorm_g, f_knorm_g, f_bias, m_conv_w, m_conv_b,
                   m_dt_bias, m_A_log, m_D, m_norm_g, w_br_r, w_br_f, w_br_m, w_out):
    w = w_in[l]
    w_main = jnp.concatenate(
        [w[:, 0:1536], w[:, 3592:4616], w[:, 4624:5136], w[:, 5136:8208], w[:, 1536:3072], w[:, 3080:3592]],
        axis=1).astype(BF16)
    zpad = jnp.zeros((D_MODEL, LANES - F_HEADS), F32)
    w_small = jnp.concatenate([w[:, 3072:3080], zpad, w[:, 4616:4624], zpad], axis=1).astype(BF16)
    return dict(
        g=norm_g[l][None, :], w_main=w_main, w_small=w_small,
        gn=r_norm_g[l][None, :],
        gq=jnp.tile(f_qnorm_g[l], F_HEADS)[None, :], gk=jnp.tile(f_knorm_g[l], F_HEADS)[None, :],
        fb=_pad_lanes(f_bias[l]),
        cw=m_conv_w[l], cb=m_conv_b[l][None, :],
        dtb=_pad_lanes(m_dt_bias[l]), alog=_pad_lanes(m_A_log[l]),
        dvec=jnp.repeat(m_D[l], M_HD)[None, :], ng=m_norm_g[l][None, :],
        wr=w_br_r[l].astype(BF16), wf=w_br_f[l].astype(BF16), wm=w_br_m[l].astype(BF16), wo=w_out[l].astype(BF16),
    )


def _consts():
    i = np.arange(F_WIDTH)
    m64 = jnp.asarray((i[:, None] // F_DH == i[None, :] // F_DH).astype(np.float32) / F_DH, dtype=BF16)
    j = np.arange(F_HEADS * LANES)
    expand = jnp.asarray((np.arange(LANES)[:, None] == j[None, :] // LANES).astype(np.float32), dtype=BF16)
    return m64, expand


def kernel(x_prompt, x_sample, cache_fox_k, cache_fox_v, cache_fox_logf, page_table, state_ret, state_conv, state_ssm, norm_g, w_in, r_norm_g, f_qnorm_g, f_knorm_g, f_bias, m_conv_w, m_conv_b, m_dt_bias, m_A_log, m_D, m_norm_g, w_br_r, w_br_f, w_br_m, w_out):
    B, L, D = x_prompt.shape
    Bd, T, _ = x_sample.shape
    depth = w_in.shape[0]
    n_pool = cache_fox_k.shape[1]
    n_pages = page_table.shape[1]
    past = n_pages * PAGE
    C = CHUNK
    n = L // C
    tq = min(L, 512)
    ppb = min(n_pages, 16)

    m64, expand = _consts()
    cos_p, sin_p = _rope_tables(jnp.arange(L))
    cos_s, sin_s = _rope_tables(past + jnp.arange(C))
    kc = jnp.transpose(cache_fox_k, (0, 1, 3, 4, 2)).reshape(depth * n_pool, F_WIDTH, PAGE)
    vc = jnp.transpose(cache_fox_v, (0, 1, 3, 4, 2)).reshape(depth * n_pool, F_WIDTH, PAGE)
    lc = jnp.transpose(cache_fox_logf, (0, 1, 3, 2)).reshape(depth * n_pool, F_HEADS, PAGE)

    hp = x_prompt.reshape(B * L, D)
    hs = jnp.pad(x_sample, ((0, 0), (0, TPAD - T), (0, 0))).reshape(Bd * TPAD, D)
    zeros_ret = jnp.zeros((B, 2, LANES, LANES), F32)
    zeros_conv = jnp.zeros((B, M_CONV - 1, M_CONV_DIM), F32)
    zeros_ssm = jnp.zeros((B, M_HEADS // 2, LANES, LANES), F32)

    outs_p, outs_s = [], []
    for l in range(depth):
        W = _layer_weights(l, norm_g, w_in, r_norm_g, f_qnorm_g, f_knorm_g, f_bias, m_conv_w, m_conv_b,
                           m_dt_bias, m_A_log, m_D, m_norm_g, w_br_r, w_br_f, w_br_m, w_out)
        z, small = _inproj(hp, W["g"], W["w_main"], W["w_small"])
        z3, small3 = z.reshape(B, L, Z_WIDTH), small.reshape(B, L, SMALL_W)
        r_out, ret_p = _retention(z3, Z_RET // 1536, cos_p, sin_p, W["gn"], zeros_ret, B=B, n=n, C=C, T=C)
        qt, kb, kft, vft, vt, lft, ccol, crow = _foxprep_prompt(
            z, small, W["gq"], W["gk"], W["fb"], m64, expand, B=B, nb=L // tq, tm=tq)
        f_out = _foxattn(qt, kb, vt, ccol, crow, z, B=B, L=L, tq=tq)
        m_out, conv_p, ssm_p = _mamba(z3, Z_MAM // 1536, small3, 1, W["cw"], W["cb"], W["dtb"], W["alog"],
                                      W["dvec"], W["ng"], zeros_conv, zeros_ssm, B=B, n=n, C=C, T=C)
        hp = _merge(hp, r_out.reshape(B * L, 512), f_out, m_out.reshape(B * L, M_WIDTH), z,
                    W["wr"], W["wf"], W["wm"], W["wo"])
        outs_p.append((kft, vft, lft, ret_p, conv_p, ssm_p))

        zs, smalls = _inproj(hs, W["g"], W["w_main"], W["w_small"])
        zs3, smalls3 = zs.reshape(Bd, TPAD, Z_WIDTH), smalls.reshape(Bd, TPAD, SMALL_W)
        r_s, ret_s = _retention(zs3, Z_RET // 1536, cos_s, sin_s, W["gn"],
                                state_ret[l].reshape(Bd, 2, LANES, LANES), B=Bd, n=1, C=C, T=T)
        qs, kf, vf, lf = _foxprep_sample(zs, smalls, W["gq"], W["gk"], W["fb"], m64)
        ck, tot = _pagecum(page_table, lc, page_off=l * n_pool)
        f_out = _decode(page_table, qs, kf, vf, lf, tot, ck, zs, kc, vc, T=T, page_off=l * n_pool, ppb=ppb)
        m_s, conv_s, ssm_s = _mamba(zs3, Z_MAM // 1536, smalls3, 1,
                                    W["cw"], W["cb"], W["dtb"], W["alog"], W["dvec"], W["ng"],
                                    state_conv[l], state_ssm[l].reshape(Bd, M_HEADS // 2, LANES, LANES),
                                    B=Bd, n=1, C=C, T=T)
        hs = _merge(hs, r_s.reshape(Bd * TPAD, 512), f_out, m_s.reshape(Bd * TPAD, M_WIDTH), zs,
                    W["wr"], W["wf"], W["wm"], W["wo"])
        sel = lambda a: a.reshape(Bd, TPAD, -1)[:, :T]
        outs_s.append((sel(kf), sel(vf), sel(lf)[:, :, :F_HEADS], ret_s, conv_s, ssm_s))

    stk = lambda outs, i: jnp.stack([o[i] for o in outs])
    return (
        hp.reshape(B, L, D),
        hs.reshape(Bd, TPAD, D)[:, :T],
        jnp.transpose(stk(outs_p, 0).reshape(depth, B, F_HEADS, F_DH, L), (0, 1, 4, 2, 3)),
        jnp.transpose(stk(outs_p, 1).reshape(depth, B, F_HEADS, F_DH, L), (0, 1, 4, 2, 3)),
        jnp.transpose(stk(outs_p, 2), (0, 1, 3, 2)),
        stk(outs_p, 3).reshape(depth, B, R_HEADS, R_DK, R_DV),
        stk(outs_p, 4),
        stk(outs_p, 5).reshape(depth, B, M_HEADS, M_HD, M_STATE),
        stk(outs_s, 0).reshape(depth, Bd, T, F_HEADS, F_DH),
        stk(outs_s, 1).reshape(depth, Bd, T, F_HEADS, F_DH),
        stk(outs_s, 2),
        stk(outs_s, 3).reshape(depth, Bd, R_HEADS, R_DK, R_DV),
        stk(outs_s, 4),
        stk(outs_s, 5).reshape(depth, Bd, M_HEADS, M_HD, M_STATE),
    )
```

```python
import functools
import math

import numpy as np
import jax
import jax.numpy as jnp
from jax import lax
from jax.experimental import pallas as pl
from jax.experimental.pallas import tpu as pltpu

F32 = jnp.float32
BF16 = jnp.bfloat16

D_MODEL = 1024
PAGE = 128
R_HEADS, R_DK, R_DV = 4, 64, 128
F_HEADS, F_DH = 8, 64
F_WIDTH = F_HEADS * F_DH
M_HEADS, M_HD, M_GROUPS, M_STATE, M_CONV = 8, 64, 2, 128, 4
M_WIDTH = M_HEADS * M_HD
M_CONV_DIM = M_WIDTH + 2 * M_GROUPS * M_STATE
CHUNK = 128
ROPE_BASE = 10000.0
EPS = 1e-6
NEG = -1e30
LOG2E = 1.4426950408889634

Z_RET, Z_MAM, Z_GATE, Z_FOX = 0, 1536, 3072, 6144
Z_WIDTH = 8192
SMALL_W = 256
LANES = 128
SUB = 8
TPAD = 8
STRIP = 256
SEQS = 4

VMEM_LIMIT = 56 * 1024 * 1024

NT_DIMS = (((1,), (1,)), ((), ()))
TN_DIMS = (((0,), (0,)), ((), ()))

LOG_G = [float(np.log1p(-np.exp2(-5.0 - h))) for h in range(R_HEADS)]


def _cparams(sem):
    return pltpu.CompilerParams(dimension_semantics=sem, vmem_limit_bytes=VMEM_LIMIT)


def _silu(x):
    return x * jax.nn.sigmoid(x)


def _softplus(x):
    return jnp.maximum(x, 0.0) + jnp.log1p(jnp.exp(-jnp.abs(x)))


def _log_sigmoid(x):
    return jnp.minimum(x, 0.0) - jnp.log1p(jnp.exp(-jnp.abs(x)))


def _bdot(a, b):
    return jnp.dot(a.astype(BF16), b.astype(BF16), preferred_element_type=F32)


def _bdot_nt(a, b):
    return lax.dot_general(a.astype(BF16), b.astype(BF16), NT_DIMS, preferred_element_type=F32)


def _bdot_tn(a, b):
    return lax.dot_general(a.astype(BF16), b.astype(BF16), TN_DIMS, preferred_element_type=F32)


def _split3(x):
    h1 = x.astype(BF16)
    r1 = x - h1.astype(F32)
    h2 = r1.astype(BF16)
    h3 = (r1 - h2.astype(F32)).astype(BF16)
    return h1, h2, h3


def _exact_dot(x, sel):
    h1, h2, h3 = _split3(x)
    d = lambda a: jnp.dot(a, sel, preferred_element_type=F32)
    return d(h1) + d(h2) + d(h3)


def _exact_rows(sel, x):
    h1, h2, h3 = _split3(x)
    d = lambda a: lax.dot_general(sel, a, NT_DIMS, preferred_element_type=F32)
    return d(h1) + d(h2) + d(h3)


def _cumsum_rows(x):
    n = x.shape[0]
    i = lax.broadcasted_iota(jnp.int32, x.shape, 0)
    s = 1
    while s < n:
        x = x + jnp.where(i >= s, pltpu.roll(x, s, 0), 0.0)
        s *= 2
    return x


def _eye_rows(r, k):
    return (lax.broadcasted_iota(jnp.int32, (r, k), 0) == lax.broadcasted_iota(jnp.int32, (r, k), 1)).astype(BF16)


def _inproj_kernel(x_ref, g_ref, w_ref, ws_ref, z_ref, s_ref, h_ref):
    @pl.when(pl.program_id(1) == 0)
    def _():
        x = x_ref[...]
        r = x * lax.rsqrt(jnp.mean(x * x, axis=-1, keepdims=True) + EPS)
        h = (r * g_ref[...]).astype(BF16)
        h_ref[...] = h
        s_ref[...] = jnp.dot(h, ws_ref[...], preferred_element_type=F32)

    z_ref[...] = jnp.dot(h_ref[...], w_ref[...], preferred_element_type=F32)


def _inproj(x, g, w_main, w_small):
    nt = x.shape[0]
    tm = next((t for t in (2048, 1024, 512, 256) if nt % t == 0), nt)
    tn = 1024
    return pl.pallas_call(
        _inproj_kernel,
        grid=(nt // tm, Z_WIDTH // tn),
        in_specs=[
            pl.BlockSpec((tm, D_MODEL), lambda i, j: (i, 0)),
            pl.BlockSpec((1, D_MODEL), lambda i, j: (0, 0)),
            pl.BlockSpec((D_MODEL, tn), lambda i, j: (0, j)),
            pl.BlockSpec((D_MODEL, SMALL_W), lambda i, j: (0, 0)),
        ],
        out_specs=[
            pl.BlockSpec((tm, tn), lambda i, j: (i, j)),
            pl.BlockSpec((tm, SMALL_W), lambda i, j: (i, 0)),
        ],
        out_shape=[jax.ShapeDtypeStruct((nt, Z_WIDTH), F32), jax.ShapeDtypeStruct((nt, SMALL_W), F32)],
        scratch_shapes=[pltpu.VMEM((tm, D_MODEL), BF16)],
        compiler_params=_cparams(("arbitrary", "arbitrary")),
        name="inproj",
    )(x, g, w_main, w_small)


def _ret_kernel(z_ref, cos_ref, sin_ref, gn_ref, s0_ref, o_ref, sl_ref, st_ref, *zp, C, T):
    c = pl.program_id(1)
    R = z_ref.shape[1]

    @pl.when(c == 0)
    def _():
        st_ref[...] = s0_ref[...]

    if R < C:
        src_ref, = zp

        @pl.when((pl.program_id(0) == 0) & (c == 0))
        def _():
            src_ref[...] = jnp.zeros_like(src_ref)

        src_ref[:, 0:R, :] = z_ref[...]
    else:
        src_ref = z_ref

    lane2 = lax.broadcasted_iota(jnp.int32, (C, 2 * LANES), 1)
    first = (lane2 % R_DK) < (R_DK // 2)
    cos = cos_ref[...]
    sin = sin_ref[...]

    def rope(x):
        xr = jnp.where(first, pltpu.roll(x, 2 * LANES - R_DK // 2, 1), pltpu.roll(x, R_DK // 2, 1))
        return x * cos + xr * sin

    ri = lax.broadcasted_iota(jnp.int32, (C, 1), 0)
    rif = ri.astype(F32)
    valid = ri < T
    di = lax.broadcasted_iota(jnp.int32, (C, C), 0)
    dj = lax.broadcasted_iota(jnp.int32, (C, C), 1)
    dd = jnp.maximum(di - dj, 0).astype(F32)
    causal = di >= dj
    lane = lax.broadcasted_iota(jnp.int32, (C, LANES), 1)
    lo = lane < R_DK
    row = lax.broadcasted_iota(jnp.int32, (LANES, LANES), 0)
    rlo = row < R_DK

    def one(g):
        zg, og, sg = src_ref.at[g], o_ref.at[g], st_ref.at[g]
        q = rope(zg[:, 0:256]) * (R_DK ** -0.5)
        k = rope(zg[:, 256:512])
        for p in range(R_HEADS // 2):
            qp = q[:, p * LANES:(p + 1) * LANES]
            kp = jnp.where(valid, k[:, p * LANES:(p + 1) * LANES], 0.0)
            sp = sg[p]
            ds = []
            for e in range(2):
                h = 2 * p + e
                lg = LOG_G[h]
                qm = jnp.where(lo if e == 0 else jnp.logical_not(lo), qp, 0.0)
                decay = jnp.where(causal, jnp.exp(lg * dd), 0.0)
                wq = jnp.exp(lg * (rif + 1.0))
                wk = jnp.exp(lg * (float(T - 1) - rif))
                vh = zg[:, 512 + h * LANES:512 + (h + 1) * LANES]
                s = _bdot_nt(qm, kp) * decay
                o = _bdot(s, vh) + _bdot(qm * wq, sp)
                mu = jnp.mean(o, axis=-1, keepdims=True)
                oc = o - mu
                var = jnp.mean(oc * oc, axis=-1, keepdims=True)
                gate = zg[:, 1024 + h * LANES:1024 + (h + 1) * LANES]
                og[:, h * LANES:(h + 1) * LANES] = (
                    oc * lax.rsqrt(var + EPS) * gn_ref[:, h * LANES:(h + 1) * LANES] * _silu(gate))[0:R].astype(og.dtype)
                ds.append(_bdot_tn(kp * wk, vh))
            gdec = jnp.where(rlo, math.exp(LOG_G[2 * p] * T), math.exp(LOG_G[2 * p + 1] * T))
            sg[p] = gdec * sp + jnp.where(rlo, ds[0], ds[1])

    for g in range(z_ref.shape[0]):
        one(g)

    @pl.when(c == pl.num_programs(1) - 1)
    def _():
        sl_ref[...] = st_ref[...]


def _retention(z, zcol, cos, sin, gn, s0, *, B, n, C, T):
    G = SEQS
    R = z.shape[1] // n
    kern = functools.partial(_ret_kernel, C=C, T=T)
    return pl.pallas_call(
        kern,
        grid=(B // G, n),
        in_specs=[
            pl.BlockSpec((G, R, 1536), lambda b, c: (b, c, zcol)),
            pl.BlockSpec((C, 256), lambda b, c: (c, 0)),
            pl.BlockSpec((C, 256), lambda b, c: (c, 0)),
            pl.BlockSpec((1, 512), lambda b, c: (0, 0)),
            pl.BlockSpec((G, 2, LANES, LANES), lambda b, c: (b, 0, 0, 0)),
        ],
        out_specs=[
            pl.BlockSpec((G, R, 512), lambda b, c: (b, c, 0)),
            pl.BlockSpec((G, 2, LANES, LANES), lambda b, c: (b, 0, 0, 0)),
        ],
        out_shape=[jax.ShapeDtypeStruct((B, n * R, 512), BF16 if R == C else F32),
                   jax.ShapeDtypeStruct((B, 2, LANES, LANES), F32)],
        scratch_shapes=[pltpu.VMEM((G, 2, LANES, LANES), F32)] + ([pltpu.VMEM((G, C, 1536), F32)] if R < C else []),
        compiler_params=_cparams(("arbitrary", "arbitrary")),
        name="retention",
    )(z, cos, sin, gn, s0)


def _mamba_kernel(z_ref, dt_ref, cw_ref, cb_ref, dtb_ref, alog_ref, dvec_ref, ng_ref, cbuf_ref, h0_ref,
                  o_ref, cnew_ref, hl_ref, xp_ref, st_ref, *, C, T):
    c = pl.program_id(1)
    K = M_CONV

    if z_ref.shape[1] < C:
        @pl.when((pl.program_id(0) == 0) & (c == 0))
        def _():
            xp_ref[...] = jnp.zeros_like(xp_ref)

    @pl.when(c == 0)
    def _():
        xp_ref[:, SUB - (K - 1):SUB, :] = cbuf_ref[...]
        st_ref[...] = h0_ref[...]

    lane = lax.broadcasted_iota(jnp.int32, (C, LANES), 1)
    ri = lax.broadcasted_iota(jnp.int32, (C, LANES), 0)
    di = lax.broadcasted_iota(jnp.int32, (C, C), 0)
    dj = lax.broadcasted_iota(jnp.int32, (C, C), 1)
    causal = di >= dj
    lo = lane < M_HD
    rlo = lax.broadcasted_iota(jnp.int32, (LANES, LANES), 0) < M_HD
    eye8 = _eye_rows(SUB, LANES)
    neg_a = -jnp.exp(alog_ref[...])

    for sq in range(z_ref.shape[0]):
        _mamba_one(z_ref.at[sq], dt_ref.at[sq], cw_ref, cb_ref, dtb_ref, dvec_ref, ng_ref, o_ref.at[sq],
                   xp_ref.at[sq], st_ref.at[sq], lane, ri, causal, lo, rlo, eye8, neg_a, C=C, T=T)

    @pl.when(c == pl.num_programs(1) - 1)
    def _():
        cnew_ref[...] = xp_ref[:, SUB - (K - 1):SUB, :]
        hl_ref[...] = st_ref[...]


def _mamba_one(z_ref, dt_ref, cw_ref, cb_ref, dtb_ref, dvec_ref, ng_ref, o_ref, xp_ref, st_ref,
               lane, ri, causal, lo, rlo, eye8, neg_a, *, C, T):
    K = M_CONV
    R = z_ref.shape[0]
    xp_ref[SUB:SUB + R, :] = z_ref[:, 0:M_CONV_DIM]
    conv = cb_ref[...]
    for j in range(K):
        conv = conv + xp_ref[SUB - (K - 1) + j:SUB - (K - 1) + j + C, :] * cw_ref[j:j + 1, :]
    tail = xp_ref[SUB + T - (K - 1):SUB + T, :]
    xp_ref[SUB - (K - 1):SUB, :] = tail
    xbc = _silu(conv)

    dt_raw = dt_ref[...]
    if R < C:
        dt_raw = jnp.concatenate([dt_raw, jnp.zeros((C - R, LANES), F32)], axis=0)
    dt = _softplus(dt_raw + dtb_ref[...])
    dt = jnp.where((lane < M_HEADS) & (ri < T), dt, 0.0)
    a = dt * neg_a
    a_cs = _cumsum_rows(a)
    a_row = _exact_rows(eye8, a_cs)
    dt_row = _exact_rows(eye8, dt)
    a_last = a_cs[C - 1:C, :]
    w_end = jnp.exp(a_last - a_cs) * dt
    ea = jnp.exp(a_cs)
    e_last = jnp.exp(a_last)

    ys = []
    heads_per_group = M_HEADS // M_GROUPS
    for g in range(M_GROUPS):
        bg = xbc[:, M_WIDTH + g * M_STATE:M_WIDTH + (g + 1) * M_STATE]
        cg = xbc[:, M_WIDTH + (M_GROUPS + g) * M_STATE:M_WIDTH + (M_GROUPS + g + 1) * M_STATE]
        cbm = _bdot_nt(cg, bg)
        for pp in range(heads_per_group // 2):
            p = g * (heads_per_group // 2) + pp
            xpair = xbc[:, p * LANES:(p + 1) * LANES]
            yh = []
            for e in range(2):
                h = 2 * p + e
                seg = a_cs[:, h:h + 1] - a_row[h:h + 1, :]
                lm = jnp.exp(jnp.where(causal, seg, NEG))
                m = cbm * lm * dt_row[h:h + 1, :]
                yh.append(_bdot(m, xpair))
            sp = st_ref[p]
            ycross = _bdot_nt(cg, sp) * jnp.where(lo, ea[:, 2 * p:2 * p + 1], ea[:, 2 * p + 1:2 * p + 2])
            y = jnp.where(lo, yh[0], yh[1]) + ycross + dvec_ref[:, p * LANES:(p + 1) * LANES] * xpair
            ys.append(y)
            xs = xpair * jnp.where(lo, w_end[:, 2 * p:2 * p + 1], w_end[:, 2 * p + 1:2 * p + 2])
            dstate = _bdot_tn(xs, bg)
            dec = jnp.where(rlo, e_last[:, 2 * p:2 * p + 1], e_last[:, 2 * p + 1:2 * p + 2])
            st_ref[p] = dec * sp + dstate

    my = jnp.concatenate(ys, axis=1)[0:R]
    gated = my * _silu(z_ref[:, M_CONV_DIM:M_CONV_DIM + M_WIDTH])
    o_ref[...] = (gated * lax.rsqrt(jnp.mean(gated * gated, axis=-1, keepdims=True) + EPS) * ng_ref[...]).astype(o_ref.dtype)


def _mamba(z, zcol, small, scol, cw, cb, dtb, alog, dvec, ng, cbuf, h0, *, B, n, C, T):
    G = SEQS
    R = z.shape[1] // n
    kern = functools.partial(_mamba_kernel, C=C, T=T)
    np_ = M_HEADS // 2
    return pl.pallas_call(
        kern,
        grid=(B // G, n),
        in_specs=[
            pl.BlockSpec((G, R, 1536), lambda b, c: (b, c, zcol)),
            pl.BlockSpec((G, R, LANES), lambda b, c: (b, c, scol)),
            pl.BlockSpec((M_CONV, M_CONV_DIM), lambda b, c: (0, 0)),
            pl.BlockSpec((1, M_CONV_DIM), lambda b, c: (0, 0)),
            pl.BlockSpec((1, LANES), lambda b, c: (0, 0)),
            pl.BlockSpec((1, LANES), lambda b, c: (0, 0)),
            pl.BlockSpec((1, M_WIDTH), lambda b, c: (0, 0)),
            pl.BlockSpec((1, M_WIDTH), lambda b, c: (0, 0)),
            pl.BlockSpec((G, M_CONV - 1, M_CONV_DIM), lambda b, c: (b, 0, 0)),
            pl.BlockSpec((G, np_, LANES, LANES), lambda b, c: (b, 0, 0, 0)),
        ],
        out_specs=[
            pl.BlockSpec((G, R, M_WIDTH), lambda b, c: (b, c, 0)),
            pl.BlockSpec((G, M_CONV - 1, M_CONV_DIM), lambda b, c: (b, 0, 0)),
            pl.BlockSpec((G, np_, LANES, LANES), lambda b, c: (b, 0, 0, 0)),
        ],
        out_shape=[
            jax.ShapeDtypeStruct((B, n * R, M_WIDTH), BF16 if R == C else F32),
            jax.ShapeDtypeStruct((B, M_CONV - 1, M_CONV_DIM), F32),
            jax.ShapeDtypeStruct((B, np_, LANES, LANES), F32),
        ],
        scratch_shapes=[pltpu.VMEM((G, SUB + C, M_CONV_DIM), F32), pltpu.VMEM((G, np_, LANES, LANES), F32)],
        compiler_params=_cparams(("arbitrary", "arbitrary")),
        name="mamba",
    )(z, small, cw, cb, dtb, alog, dvec, ng, cbuf, h0)


def _fox_qkv(z_ref, ff_ref, gq_ref, gk_ref, fb_ref, m64_ref):
    m64 = m64_ref[...]

    def headnorm(x, g):
        x2 = x * x
        hi = x2.astype(BF16)
        lo = (x2 - hi.astype(F32)).astype(BF16)
        ms = jnp.dot(hi, m64, preferred_element_type=F32) + jnp.dot(lo, m64, preferred_element_type=F32)
        return x * lax.rsqrt(ms + EPS) * g

    qs = headnorm(z_ref[:, 0:F_WIDTH], gq_ref[...]) * (F_DH ** -0.5)
    kn = headnorm(z_ref[:, F_WIDTH:2 * F_WIDTH], gk_ref[...])
    v = z_ref[:, 2 * F_WIDTH:3 * F_WIDTH]
    lane = lax.broadcasted_iota(jnp.int32, ff_ref.shape, 1)
    logf = jnp.where(lane < F_HEADS, _log_sigmoid(ff_ref[...] + fb_ref[...]), 0.0)
    return qs, kn, v, logf


def _foxprep_prompt_kernel(z_ref, ff_ref, gq_ref, gk_ref, fb_ref, m64_ref, exp_ref,
                           qt_ref, kb_ref, kft_ref, vft_ref, vt_ref, lft_ref, ccol_ref, crow_ref, carry_ref, *, tm):
    qs, kn, v, logf = _fox_qkv(z_ref, ff_ref, gq_ref, gk_ref, fb_ref, m64_ref)
    kb_ref[...] = kn.astype(BF16)
    eye8 = _eye_rows(SUB, LANES)
    lft_ref[...] = _exact_rows(eye8, logf)
    qs2 = qs * LOG2E
    for p in range(F_HEADS // 2):
        cols = slice(p * LANES, (p + 1) * LANES)
        qt_ref[p] = jnp.transpose(qs2[:, cols]).astype(BF16)
        kft_ref[cols, :] = jnp.transpose(kn[:, cols])
        vtp = jnp.transpose(v[:, cols])
        vft_ref[cols, :] = vtp
        vt_ref[p] = vtp.astype(BF16)

    @pl.when(pl.program_id(1) == 0)
    def _():
        carry_ref[...] = jnp.zeros_like(carry_ref)

    cs = _cumsum_rows(logf) + carry_ref[...]
    carry_ref[...] = cs[tm - 1:tm, :]
    cs2 = cs * LOG2E
    ccol_ref[...] = _exact_dot(cs2, exp_ref[...])
    crow = _exact_rows(eye8, cs2)
    for p in range(F_HEADS // 2):
        crow_ref[p] = crow[2 * p:2 * p + 2, :]


def _foxprep_prompt(z, small, gq, gk, fb, m64, expand, *, B, nb, tm):
    nt = B * nb * tm
    kern = functools.partial(_foxprep_prompt_kernel, tm=tm)
    zc = Z_FOX // 2048
    npair = F_HEADS // 2
    row = lambda b, i: (b * nb + i, 0)
    const = lambda b, i: (0, 0)
    return pl.pallas_call(
        kern,
        grid=(B, nb),
        in_specs=[
            pl.BlockSpec((tm, 2048), lambda b, i: (b * nb + i, zc)),
            pl.BlockSpec((tm, LANES), row),
            pl.BlockSpec((1, F_WIDTH), const),
            pl.BlockSpec((1, F_WIDTH), const),
            pl.BlockSpec((1, LANES), const),
            pl.BlockSpec((F_WIDTH, F_WIDTH), const),
            pl.BlockSpec((LANES, F_HEADS * LANES), const),
        ],
        out_specs=[
            pl.BlockSpec((None, npair, None, LANES, tm), lambda b, i: (b, 0, i, 0, 0)),
            pl.BlockSpec((tm, F_WIDTH), row),
            pl.BlockSpec((None, F_WIDTH, tm), lambda b, i: (b, 0, i)),
            pl.BlockSpec((None, F_WIDTH, tm), lambda b, i: (b, 0, i)),
            pl.BlockSpec((None, npair, None, LANES, tm), lambda b, i: (b, 0, i, 0, 0)),
            pl.BlockSpec((None, F_HEADS, tm), lambda b, i: (b, 0, i)),
            pl.BlockSpec((tm, F_HEADS * LANES), row),
            pl.BlockSpec((None, None, npair, 2, tm), lambda b, i: (b, i, 0, 0, 0)),
        ],
        out_shape=[
            jax.ShapeDtypeStruct((B, npair, nb, LANES, tm), BF16),
            jax.ShapeDtypeStruct((nt, F_WIDTH), BF16),
            jax.ShapeDtypeStruct((B, F_WIDTH, nb * tm), F32),
            jax.ShapeDtypeStruct((B, F_WIDTH, nb * tm), F32),
            jax.ShapeDtypeStruct((B, npair, nb, LANES, tm), BF16),
            jax.ShapeDtypeStruct((B, F_HEADS, nb * tm), F32),
            jax.ShapeDtypeStruct((nt, F_HEADS * LANES), F32),
            jax.ShapeDtypeStruct((B, nb, npair, 2, tm), F32),
        ],
        scratch_shapes=[pltpu.VMEM((1, LANES), F32)],
        compiler_params=_cparams(("arbitrary", "arbitrary")),
        name="foxprep_prompt",
    )(z, small, gq, gk, fb, m64, expand)


def _foxprep_sample_kernel(z_ref, ff_ref, gq_ref, gk_ref, fb_ref, m64_ref, qs_ref, kf_ref, vf_ref, lf_ref):
    qs, kn, v, logf = _fox_qkv(z_ref, ff_ref, gq_ref, gk_ref, fb_ref, m64_ref)
    qs_ref[...] = qs
    kf_ref[...] = kn
    vf_ref[...] = v
    lf_ref[...] = logf


def _foxprep_sample(z, small, gq, gk, fb, m64):
    nt = z.shape[0]
    zc = Z_FOX // 2048
    row = lambda i: (i, 0)
    const = lambda i: (0, 0)
    return pl.pallas_call(
        _foxprep_sample_kernel,
        grid=(1,),
        in_specs=[
            pl.BlockSpec((nt, 2048), lambda i: (i, zc)),
            pl.BlockSpec((nt, LANES), row),
            pl.BlockSpec((1, F_WIDTH), const),
            pl.BlockSpec((1, F_WIDTH), const),
            pl.BlockSpec((1, LANES), const),
            pl.BlockSpec((F_WIDTH, F_WIDTH), const),
        ],
        out_specs=[
            pl.BlockSpec((nt, F_WIDTH), row),
            pl.BlockSpec((nt, F_WIDTH), row),
            pl.BlockSpec((nt, F_WIDTH), row),
            pl.BlockSpec((nt, LANES), row),
        ],
        out_shape=[
            jax.ShapeDtypeStruct((nt, F_WIDTH), F32),
            jax.ShapeDtypeStruct((nt, F_WIDTH), F32),
            jax.ShapeDtypeStruct((nt, F_WIDTH), F32),
            jax.ShapeDtypeStruct((nt, LANES), F32),
        ],
        compiler_params=_cparams(("arbitrary",)),
        name="foxprep_sample",
    )(z, small, gq, gk, fb, m64)


def _foxattn_kernel(qt_ref, k_ref, vt_ref, cq_ref, cke_ref, cko_ref, fg_ref, o_ref, m_ref, l_ref, acc_ref, s_ref,
                    *, tq):
    qi = pl.program_id(2)
    nq = pl.num_programs(2)
    rlo = lax.broadcasted_iota(jnp.int32, (LANES, tq), 0) < F_DH

    def two_heads(qt):
        zero = jnp.zeros_like(qt)
        return jnp.concatenate([jnp.where(rlo, qt, zero), jnp.where(rlo, zero, qt)], axis=1)

    q2t = two_heads(qt_ref[qi])
    q2t_next = two_heads(qt_ref[jnp.minimum(qi + 1, nq - 1)])
    cq2 = jnp.concatenate([cq_ref[0:1, :], cq_ref[1:2, :]], axis=1)
    m_ref[...] = jnp.full_like(m_ref, NEG)
    l_ref[...] = jnp.zeros_like(l_ref)
    acc_ref[...] = jnp.zeros_like(acc_ref)
    kpos = lax.broadcasted_iota(jnp.int32, (tq, 2 * tq), 0)
    qpos = lax.broadcasted_iota(jnp.int32, (tq, 2 * tq), 1) % tq
    causal = kpos <= qpos

    strips = list(range(0, 2 * tq, STRIP))

    def keys(j):
        return k_ref[pl.ds(pl.multiple_of(j * tq, tq), tq), :]

    def qk(k, q, c0):
        return jnp.dot(k, q[:, c0:c0 + STRIP], preferred_element_type=F32)

    def block(j, masked):
        slot = j % 2
        nslot = 0 if masked else 1 - slot
        start = pl.multiple_of(j * tq, tq)
        vt = vt_ref[j]
        cks = (cke_ref[pl.ds(start, tq), :], cko_ref[pl.ds(start, tq), :])
        k_next = keys(0) if masked else keys(j + 1)
        q_next = q2t_next if masked else q2t
        m_all, l_all, acc_all = m_ref[...], l_ref[...], acc_ref[...]
        m_out, l_out, acc_out = [], [], []
        for c0 in strips:
            cols = slice(c0, c0 + STRIP)
            st = s_ref[slot, :, cols]
            nxt = qk(k_next, q_next, c0)
            u = st - jnp.concatenate([cks[c0 // tq]] * (STRIP // LANES), axis=1)
            if masked:
                u = jnp.where(causal[:, cols], u, NEG)
            cq = cq2[:, cols]
            m_prev = m_all[:, cols]
            m_new = jnp.maximum(m_prev, jnp.max(u, axis=0, keepdims=True) + cq)
            alpha = jnp.exp2(m_prev - m_new)
            p = jnp.exp2(u + (cq - m_new))
            l_out.append(alpha * l_all[:, cols] + jnp.sum(p, axis=0, keepdims=True))
            acc_out.append(alpha * acc_all[:, cols] + jnp.dot(vt, p.astype(BF16), preferred_element_type=F32))
            m_out.append(m_new)
            s_ref[nslot, :, cols] = nxt
        m_ref[...] = jnp.concatenate(m_out, axis=1)
        l_ref[...] = jnp.concatenate(l_out, axis=1)
        acc_ref[...] = jnp.concatenate(acc_out, axis=1)

    @pl.when(qi == 0)
    def _():
        k0 = keys(0)
        for c0 in strips:
            s_ref[0, :, c0:c0 + STRIP] = qk(k0, q2t, c0)

    def body(j, carry):
        block(j, False)
        return carry

    lax.fori_loop(0, qi, body, 0)
    block(qi, True)
    on = acc_ref[...] / l_ref[...]
    ot = jnp.concatenate([on[0:F_DH, 0:tq], on[F_DH:2 * F_DH, tq:2 * tq]], axis=0)
    o_ref[...] = (jnp.transpose(ot) * _silu(fg_ref[...])).astype(o_ref.dtype)


def _foxattn(qt, kb, vt, ccol, crow, z, *, B, L, tq):
    nq = L // tq
    npair = F_HEADS // 2
    fgc = (Z_FOX + 3 * F_WIDTH) // LANES
    kern = functools.partial(_foxattn_kernel, tq=tq)
    return pl.pallas_call(
        kern,
        grid=(B, npair, nq),
        in_specs=[
            pl.BlockSpec((None, None, nq, LANES, tq), lambda b, p, i: (b, p, 0, 0, 0)),
            pl.BlockSpec((L, LANES), lambda b, p, i: (b, p)),
            pl.BlockSpec((None, None, nq, LANES, tq), lambda b, p, i: (b, p, 0, 0, 0)),
            pl.BlockSpec((None, None, None, 2, tq), lambda b, p, i: (b, i, p, 0, 0)),
            pl.BlockSpec((L, LANES), lambda b, p, i: (b, 2 * p)),
            pl.BlockSpec((L, LANES), lambda b, p, i: (b, 2 * p + 1)),
            pl.BlockSpec((tq, LANES), lambda b, p, i: (b * nq + i, fgc + p)),
        ],
        out_specs=pl.BlockSpec((tq, LANES), lambda b, p, i: (b * nq + i, p)),
        out_shape=jax.ShapeDtypeStruct((B * L, F_WIDTH), BF16),
        scratch_shapes=[pltpu.VMEM((1, 2 * tq), F32), pltpu.VMEM((1, 2 * tq), F32), pltpu.VMEM((LANES, 2 * tq), F32),
                        pltpu.VMEM((2, tq, 2 * tq), F32)],
        compiler_params=_cparams(("arbitrary", "arbitrary", "arbitrary")),
        name="foxattn",
    )(qt, kb, vt, crow, ccol, ccol, z)


def _pagecum_kernel(pt_ref, lf_hbm, ck_ref, tot_ref, g_ref, sem, *, n_pages, page_off):
    b = pl.program_id(0)

    def copy(bb, pg, slot):
        return pltpu.make_async_copy(lf_hbm.at[pt_ref[bb, pg] + page_off], g_ref.at[slot, pg], sem.at[slot])

    def start_all(bb, slot):
        def body(pg, c):
            copy(bb, pg, slot).start()
            return c
        lax.fori_loop(0, n_pages, body, 0)

    @pl.when(b == 0)
    def _():
        start_all(b, 0)

    @pl.when(b + 1 < pl.num_programs(0))
    def _():
        start_all(b + 1, (b + 1) % 2)

    slot = b % 2

    def wait(pg, c):
        copy(b, pg, slot).wait()
        return c

    lax.fori_loop(0, n_pages, wait, 0)

    rows = n_pages * F_HEADS
    w = g_ref[slot].reshape(rows, PAGE)
    row = lax.broadcasted_iota(jnp.int32, (rows, PAGE), 0)
    tri = (lax.broadcasted_iota(jnp.int32, (PAGE, PAGE), 0) <= lax.broadcasted_iota(jnp.int32, (PAGE, PAGE), 1)).astype(BF16)
    ones = jnp.ones((PAGE, PAGE), BF16)
    parts = _split3(w)
    w = sum(jnp.dot(h, tri, preferred_element_type=F32) for h in parts)
    tot = sum(jnp.dot(h, ones, preferred_element_type=F32) for h in parts)
    inc = tot
    s = F_HEADS
    while s < rows:
        inc = inc + jnp.where(row >= s, pltpu.roll(inc, s, 0), 0.0)
        s *= 2
    ck_ref[...] = (w + (inc - tot)).reshape(n_pages, F_HEADS, PAGE)
    last = inc[rows - F_HEADS:rows, :]
    eye = (lax.broadcasted_iota(jnp.int32, (F_HEADS, LANES), 0) == lax.broadcasted_iota(jnp.int32, (F_HEADS, LANES), 1))
    tot_ref[...] = jnp.sum(jnp.where(eye, last, 0.0), axis=0, keepdims=True)


def _pagecum(page_table, lf_cache, *, page_off):
    bd, n_pages = page_table.shape
    kern = functools.partial(_pagecum_kernel, n_pages=n_pages, page_off=page_off)
    return pl.pallas_call(
        kern,
        grid_spec=pltpu.PrefetchScalarGridSpec(
            num_scalar_prefetch=1,
            grid=(bd,),
            in_specs=[pl.BlockSpec(memory_space=pl.ANY)],
            out_specs=[
                pl.BlockSpec((None, n_pages, F_HEADS, PAGE), lambda b, pt: (b, 0, 0, 0)),
                pl.BlockSpec((None, 1, LANES), lambda b, pt: (b, 0, 0)),
            ],
            scratch_shapes=[pltpu.VMEM((2, n_pages, F_HEADS, PAGE), F32), pltpu.SemaphoreType.DMA((2,))],
        ),
        out_shape=[
            jax.ShapeDtypeStruct((bd, n_pages, F_HEADS, PAGE), F32),
            jax.ShapeDtypeStruct((bd, 1, LANES), F32),
        ],
        compiler_params=_cparams(("arbitrary",)),
        name="pagecum",
    )(page_table, lf_cache)


def _decode_kernel(pt_ref, q_ref, kn_ref, vn_ref, lfn_ref, tot_ref, ck_ref, fg_ref, k_hbm, v_hbm,
                   o_ref, kbuf, vbuf, sem, qbd_ref, cq_ref, cn_ref, m_ref, l_ref, acc_ref,
                   *, T, ppb, ns, page_off):
    b = pl.program_id(0)
    s = pl.program_id(1)
    t = b * ns + s
    nt = pl.num_programs(0) * ns
    R = T * F_HEADS

    def copies(bb, ss, slot):
        out = []
        for i in range(ppb):
            page = pt_ref[bb, ss * ppb + i] + page_off
            out.append(pltpu.make_async_copy(k_hbm.at[page], kbuf.at[slot, i], sem.at[0, slot]))
            out.append(pltpu.make_async_copy(v_hbm.at[page], vbuf.at[slot, i], sem.at[1, slot]))
        return out

    @pl.when(t == 0)
    def _():
        for cp in copies(b, s, 0):
            cp.start()

    @pl.when(t + 1 < nt)
    def _():
        t1 = t + 1
        for cp in copies(t1 // ns, t1 % ns, t1 % 2):
            cp.start()

    rl = lax.broadcasted_iota(jnp.int32, (SUB, F_WIDTH), 1) // F_DH
    rh = lax.broadcasted_iota(jnp.int32, (SUB, F_WIDTH), 0)
    hmask = rl == rh

    @pl.when(s == 0)
    def _():
        q = q_ref[...].astype(F32)
        qbd_ref[...] = jnp.concatenate(
            [jnp.where(hmask, jnp.broadcast_to(q[tt:tt + 1, :], (SUB, F_WIDTH)), 0.0) for tt in range(T)],
            axis=0).astype(BF16)
        cn = tot_ref[...] + _cumsum_rows(lfn_ref[...])
        cn_ref[...] = cn
        eye = (lax.broadcasted_iota(jnp.int32, (SUB, LANES), 0) == lax.broadcasted_iota(jnp.int32, (SUB, LANES), 1))
        cq_ref[...] = jnp.concatenate(
            [jnp.sum(jnp.where(eye, jnp.broadcast_to(cn[tt:tt + 1, :], (SUB, LANES)), 0.0), axis=-1, keepdims=True)
             for tt in range(T)], axis=0)
        m_ref[...] = jnp.full_like(m_ref, NEG)
        l_ref[...] = jnp.zeros_like(l_ref)
        acc_ref[...] = jnp.zeros_like(acc_ref)

    slot = t % 2
    for cp in copies(b, s, slot):
        cp.wait()

    def update(sc, v):
        m_prev = m_ref[...]
        m_new = jnp.maximum(m_prev, jnp.max(sc, axis=-1, keepdims=True))
        alpha = jnp.exp(m_prev - m_new)
        p = jnp.exp(sc - m_new)
        l_ref[...] = alpha * l_ref[...] + jnp.sum(p, axis=-1, keepdims=True)
        acc_ref[...] = alpha * acc_ref[...] + jnp.dot(p.astype(BF16), v, preferred_element_type=F32)
        m_ref[...] = m_new

    qbd = qbd_ref[...]
    cq = cq_ref[...]
    sc = jnp.concatenate(
        [jnp.dot(qbd, kbuf[slot, i].astype(BF16), preferred_element_type=F32) for i in range(ppb)], axis=1)
    ck = jnp.concatenate([ck_ref[i] for i in range(ppb)], axis=1)
    sc = sc + cq - jnp.concatenate([ck] * T, axis=0)
    m_prev = m_ref[...]
    m_new = jnp.maximum(m_prev, jnp.max(sc, axis=-1, keepdims=True))
    alpha = jnp.exp(m_prev - m_new)
    pf = jnp.exp(sc - m_new)
    l_ref[...] = alpha * l_ref[...] + jnp.sum(pf, axis=-1, keepdims=True)
    p = pf.astype(BF16)
    pv =lax.dot_general(p[:, 0:PAGE], vbuf[slot, 0].astype(BF16), NT_DIMS, preferred_element_type=F32)
    for i in range(1, ppb):
        pv = pv + lax.dot_general(p[:, i * PAGE:(i + 1) * PAGE], vbuf[slot, i].astype(BF16), NT_DIMS,
                                  preferred_element_type=F32)
    acc_ref[...] = alpha * acc_ref[...] + pv
    m_ref[...] = m_new

    @pl.when(s == ns - 1)
    def _():
        cn = cn_ref[...]
        cn_pad = jnp.concatenate([cn, jnp.zeros((PAGE - SUB, LANES), F32)], axis=0)
        cn_row = _exact_rows(_eye_rows(SUB, LANES), cn_pad)
        zpad = jnp.zeros((PAGE - TPAD, F_WIDTH), F32)
        kn = jnp.concatenate([kn_ref[...], zpad], axis=0).astype(BF16)
        vn = jnp.concatenate([vn_ref[...], zpad], axis=0).astype(BF16)
        sn = lax.dot_general(qbd, kn, NT_DIMS, preferred_element_type=F32)
        sn = sn + cq - jnp.concatenate([cn_row] * T, axis=0)
        rt = lax.broadcasted_iota(jnp.int32, (R, PAGE), 0) // F_HEADS
        ct = lax.broadcasted_iota(jnp.int32, (R, PAGE), 1)
        sn = jnp.where((ct <= rt) & (ct < T), sn, NEG)
        update(sn, vn)
        on = acc_ref[...] / l_ref[...]
        for tt in range(T):
            blk = jnp.where(hmask, on[tt * F_HEADS:(tt + 1) * F_HEADS, :], 0.0)
            o_ref[tt:tt + 1, :] = jnp.sum(blk, axis=0, keepdims=True) * _silu(fg_ref[tt:tt + 1, :])
        o_ref[T:SUB, :] = jnp.zeros((SUB - T, F_WIDTH), F32)


def _decode(page_table, qb, kn_pad, vn_pad, lf, tot, ck, z, k_cache, v_cache, *, T, page_off, ppb):
    bd, n_pages = page_table.shape
    ns = n_pages // ppb
    R = T * F_HEADS
    fgc = (Z_FOX + 3 * F_WIDTH) // F_WIDTH
    kern = functools.partial(_decode_kernel, T=T, ppb=ppb, ns=ns, page_off=page_off)
    return pl.pallas_call(
        kern,
        grid_spec=pltpu.PrefetchScalarGridSpec(
            num_scalar_prefetch=1,
            grid=(bd, ns),
            in_specs=[
                pl.BlockSpec((TPAD, F_WIDTH), lambda b, s, pt: (b, 0)),
                pl.BlockSpec((TPAD, F_WIDTH), lambda b, s, pt: (b, 0)),
                pl.BlockSpec((TPAD, F_WIDTH), lambda b, s, pt: (b, 0)),
                pl.BlockSpec((TPAD, LANES), lambda b, s, pt: (b, 0)),
                pl.BlockSpec((None, 1, LANES), lambda b, s, pt: (b, 0, 0)),
                pl.BlockSpec((None, ppb, F_HEADS, PAGE), lambda b, s, pt: (b, s, 0, 0)),
                pl.BlockSpec((TPAD, F_WIDTH), lambda b, s, pt: (b, fgc)),
                pl.BlockSpec(memory_space=pl.ANY),
                pl.BlockSpec(memory_space=pl.ANY),
            ],
            out_specs=pl.BlockSpec((TPAD, F_WIDTH), lambda b, s, pt: (b, 0)),
            scratch_shapes=[
                pltpu.VMEM((2, ppb, F_WIDTH, PAGE), F32),
                pltpu.VMEM((2, ppb, F_WIDTH, PAGE), F32),
                pltpu.SemaphoreType.DMA((2, 2)),
                pltpu.VMEM((R, F_WIDTH), BF16),
                pltpu.VMEM((R, 1), F32),
                pltpu.VMEM((SUB, LANES), F32),
                pltpu.VMEM((R, 1), F32),
                pltpu.VMEM((R, 1), F32),
                pltpu.VMEM((R, F_WIDTH), F32),
            ],
        ),
        out_shape=jax.ShapeDtypeStruct((bd * TPAD, F_WIDTH), F32),
        compiler_params=_cparams(("arbitrary", "arbitrary")),
        name="decode",
    )(page_table, qb, kn_pad, vn_pad, lf, tot, ck, z, k_cache, v_cache)


def _merge_kernel(x_ref, r_ref, f_ref, m_ref, gate_ref, wr_ref, wf_ref, wm_ref, wo_ref, y_ref):
    br = _bdot(r_ref[...], wr_ref[...])
    bf = _bdot(f_ref[...], wf_ref[...])
    bm = _bdot(m_ref[...], wm_ref[...])
    merged = (jax.nn.sigmoid(gate_ref[:, 0:D_MODEL]) * br
              + jax.nn.sigmoid(gate_ref[:, D_MODEL:2 * D_MODEL]) * bf
              + jax.nn.sigmoid(gate_ref[:, 2 * D_MODEL:3 * D_MODEL]) * bm)
    y_ref[...] = x_ref[...] + _bdot(merged, wo_ref[...])


def _merge(x, r, f, m, z, wr, wf, wm, wo):
    nt = x.shape[0]
    tm = min(nt, 512)
    row = lambda i: (i, 0)
    full = lambda i: (0, 0)
    return pl.pallas_call(
        _merge_kernel,
        grid=(nt // tm,),
        in_specs=[
            pl.BlockSpec((tm, D_MODEL), row),
            pl.BlockSpec((tm, 512), row),
            pl.BlockSpec((tm, 512), row),
            pl.BlockSpec((tm, 512), row),
            pl.BlockSpec((tm, 3 * D_MODEL), lambda i: (i, Z_GATE // (3 * D_MODEL))),
            pl.BlockSpec((512, D_MODEL), full),
            pl.BlockSpec((512, D_MODEL), full),
            pl.BlockSpec((512, D_MODEL), full),
            pl.BlockSpec((D_MODEL, D_MODEL), full),
        ],
        out_specs=pl.BlockSpec((tm, D_MODEL), row),
        out_shape=jax.ShapeDtypeStruct((nt, D_MODEL), F32),
        compiler_params=_cparams(("arbitrary",)),
        name="merge",
    )(x, r, f, m, z, wr, wf, wm, wo)


def _rope_tables(pos):
    inv = ROPE_BASE ** (-jnp.arange(0, R_DK, 2, dtype=F32) / R_DK)
    ang = pos.astype(F32)[:, None] * inv[None, :]
    cos, sin = jnp.cos(ang), jnp.sin(ang)
    cos_h = jnp.concatenate([cos, cos], axis=1)
    sin_h = jnp.concatenate([-sin, sin], axis=1)
    return jnp.tile(cos_h, (1, R_HEADS)), jnp.tile(sin_h, (1, R_HEADS))


def _pad_lanes(v, off=0, width=LANES):
    return jnp.zeros((1, width), F32).at[0, off:off + v.shape[0]].set(v)


def _layer_weights(l, norm_g, w_in, r_norm_g, f_qnorm_g, f_knorm_g, f_bias, m_conv_w, m_conv_b,
                   m_dt_bias, m_A_log, m_D, m_norm_g, w_br_r, w_br_f, w_br_m, w_out):
    w = w_in[l]
    w_main = jnp.concatenate(
        [w[:, 0:1536], w[:, 3592:4616], w[:, 4624:5136], w[:, 5136:8208], w[:, 1536:3072], w[:, 3080:3592]],
        axis=1).astype(BF16)
    zpad = jnp.zeros((D_MODEL, LANES - F_HEADS), F32)
    w_small = jnp.concatenate([w[:, 3072:3080], zpad, w[:, 4616:4624], zpad], axis=1).astype(BF16)
    return dict(
        g=norm_g[l][None, :], w_main=w_main, w_small=w_small,
        gn=r_norm_g[l][None, :],
        gq=jnp.tile(f_qnorm_g[l], F_HEADS)[None, :], gk=jnp.tile(f_knorm_g[l], F_HEADS)[None, :],
        fb=_pad_lanes(f_bias[l]),
        cw=m_conv_w[l], cb=m_conv_b[l][None, :],
        dtb=_pad_lanes(m_dt_bias[l]), alog=_pad_lanes(m_A_log[l]),
        dvec=jnp.repeat(m_D[l], M_HD)[None, :], ng=m_norm_g[l][None, :],
        wr=w_br_r[l].astype(BF16), wf=w_br_f[l].astype(BF16), wm=w_br_m[l].astype(BF16), wo=w_out[l].astype(BF16),
    )


def _consts():
    i = np.arange(F_WIDTH)
    m64 = jnp.asarray((i[:, None] // F_DH == i[None, :] // F_DH).astype(np.float32) / F_DH, dtype=BF16)
    j = np.arange(F_HEADS * LANES)
    expand = jnp.asarray((np.arange(LANES)[:, None] == j[None, :] // LANES).astype(np.float32), dtype=BF16)
    return m64, expand


def kernel(x_prompt, x_sample, cache_fox_k, cache_fox_v, cache_fox_logf, page_table, state_ret, state_conv, state_ssm, norm_g, w_in, r_norm_g, f_qnorm_g, f_knorm_g, f_bias, m_conv_w, m_conv_b, m_dt_bias, m_A_log, m_D, m_norm_g, w_br_r, w_br_f, w_br_m, w_out):
    B, L, D = x_prompt.shape
    Bd, T, _ = x_sample.shape
    depth = w_in.shape[0]
    n_pool = cache_fox_k.shape[1]
    n_pages = page_table.shape[1]
    past = n_pages * PAGE
    C = CHUNK
    n = L // C
    tq = min(L, 512)
    ppb = min(n_pages, 32)

    m64, expand = _consts()
    cos_p, sin_p = _rope_tables(jnp.arange(L))
    cos_s, sin_s = _rope_tables(past + jnp.arange(C))
    kc = jnp.transpose(cache_fox_k, (0, 1, 3, 4, 2)).reshape(depth * n_pool, F_WIDTH, PAGE)
    vc = jnp.transpose(cache_fox_v, (0, 1, 3, 4, 2)).reshape(depth * n_pool, F_WIDTH, PAGE)
    lc = jnp.transpose(cache_fox_logf, (0, 1, 3, 2)).reshape(depth * n_pool, F_HEADS, PAGE)

    hp = x_prompt.reshape(B * L, D)
    hs = jnp.pad(x_sample, ((0, 0), (0, TPAD - T), (0, 0))).reshape(Bd * TPAD, D)
    zeros_ret = jnp.zeros((B, 2, LANES, LANES), F32)
    zeros_conv = jnp.zeros((B, M_CONV - 1, M_CONV_DIM), F32)
    zeros_ssm = jnp.zeros((B, M_HEADS // 2, LANES, LANES), F32)

    outs_p, outs_s = [], []
    for l in range(depth):
        W = _layer_weights(l, norm_g, w_in, r_norm_g, f_qnorm_g, f_knorm_g, f_bias, m_conv_w, m_conv_b,
                           m_dt_bias, m_A_log, m_D, m_norm_g, w_br_r, w_br_f, w_br_m, w_out)
        z, small = _inproj(hp, W["g"], W["w_main"], W["w_small"])
        z3, small3 = z.reshape(B, L, Z_WIDTH), small.reshape(B, L, SMALL_W)
        r_out, ret_p = _retention(z3, Z_RET // 1536, cos_p, sin_p, W["gn"], zeros_ret, B=B, n=n, C=C, T=C)
        qt, kb, kft, vft, vt, lft, ccol, crow = _foxprep_prompt(
            z, small, W["gq"], W["gk"], W["fb"], m64, expand, B=B, nb=L // tq, tm=tq)
        f_out = _foxattn(qt, kb, vt, ccol, crow, z, B=B, L=L, tq=tq)
        m_out, conv_p, ssm_p = _mamba(z3, Z_MAM // 1536, small3, 1, W["cw"], W["cb"], W["dtb"], W["alog"],
                                      W["dvec"], W["ng"], zeros_conv, zeros_ssm, B=B, n=n, C=C, T=C)
        hp = _merge(hp, r_out.reshape(B * L, 512), f_out, m_out.reshape(B * L, M_WIDTH), z,
                    W["wr"], W["wf"], W["wm"], W["wo"])
        outs_p.append((kft, vft, lft, ret_p, conv_p, ssm_p))

        zs, smalls = _inproj(hs, W["g"], W["w_main"], W["w_small"])
        zs3, smalls3 = zs.reshape(Bd, TPAD, Z_WIDTH), smalls.reshape(Bd, TPAD, SMALL_W)
        r_s, ret_s = _retention(zs3, Z_RET // 1536, cos_s, sin_s, W["gn"],
                                state_ret[l].reshape(Bd, 2, LANES, LANES), B=Bd, n=1, C=C, T=T)
        qs, kf, vf, lf = _foxprep_sample(zs, smalls, W["gq"], W["gk"], W["fb"], m64)
        ck, tot = _pagecum(page_table, lc, page_off=l * n_pool)
        f_out = _decode(page_table, qs, kf, vf, lf, tot, ck, zs, kc, vc, T=T, page_off=l * n_pool, ppb=ppb)
        m_s, conv_s, ssm_s = _mamba(zs3, Z_MAM // 1536, smalls3, 1,
                                    W["cw"], W["cb"], W["dtb"], W["alog"], W["dvec"], W["ng"],
                                    state_conv[l], state_ssm[l].reshape(Bd, M_HEADS // 2, LANES, LANES),
                                    B=Bd, n=1, C=C, T=T)
        hs = _merge(hs, r_s.reshape(Bd * TPAD, 512), f_out, m_s.reshape(Bd * TPAD, M_WIDTH), zs,
                    W["wr"], W["wf"], W["wm"], W["wo"])
        sel = lambda a: a.reshape(Bd, TPAD, -1)[:, :T]
        outs_s.append((sel(kf), sel(vf), sel(lf)[:, :, :F_HEADS], ret_s, conv_s, ssm_s))

    stk = lambda outs, i: jnp.stack([o[i] for o in outs])
    return (
        hp.reshape(B, L, D),
        hs.reshape(Bd, TPAD, D)[:, :T],
        jnp.transpose(stk(outs_p, 0).reshape(depth, B, F_HEADS, F_DH, L), (0, 1, 4, 2, 3)),
        jnp.transpose(stk(outs_p, 1).reshape(depth, B, F_HEADS, F_DH, L), (0, 1, 4, 2, 3)),
        jnp.transpose(stk(outs_p, 2), (0, 1, 3, 2)),
        stk(outs_p, 3).reshape(depth, B, R_HEADS, R_DK, R_DV),
        stk(outs_p, 4),
        stk(outs_p, 5).reshape(depth, B, M_HEADS, M_HD, M_STATE),
        stk(outs_s, 0).reshape(depth, Bd, T, F_HEADS, F_DH),
        stk(outs_s, 1).reshape(depth, Bd, T, F_HEADS, F_DH),
        stk(outs_s, 2),
        stk(outs_s, 3).reshape(depth, Bd, R_HEADS, R_DK, R_DV),
        stk(outs_s, 4),
        stk(outs_s, 5).reshape(depth, Bd, M_HEADS, M_HD, M_STATE),
    )
```
